```python
import jax, jax.numpy as jnp
from jax import lax
import numpy as np

D_MODEL = 2048
BATCH = 4
SEQ = 2048
DEPTH = 2
DEC_BATCH = 32
DEC_SEQ = 4
PAST_LEN = 8192
PAGE_SIZE = 128

HEAD_DIM = 128
HEADS_PER_GROUP = 4
ATTN_GROUPS = ((128, 1), (512, 4), (2048, 16))
N_ATTN_HEADS = HEADS_PER_GROUP * len(ATTN_GROUPS)
ATTN_WIDTH = N_ATTN_HEADS * HEAD_DIM
ATTN_OUT_WIDTH = HEADS_PER_GROUP * HEAD_DIM
ROPE_THETA = 10000.0
CONV_CH = D_MODEL // 2
DW_CONV_LEN = 31
POOL_WINDOWS = (2, 4, 8, 16)
N_POOL_GROUPS = len(POOL_WINDOWS)
POOL_GROUP = D_MODEL // 8
POOL_WIDTH = N_POOL_GROUPS * POOL_GROUP
POOL_OUT_GROUP = D_MODEL // N_POOL_GROUPS
MAX_POOL = max(POOL_WINDOWS)
N_BRANCHES = 3
IN_WIDTH = 2 * CONV_CH + 3 * ATTN_WIDTH + POOL_WIDTH + N_BRANCHES * D_MODEL
N_EXPERT_GROUPS = 4
EXPERTS_PER_GROUP = 4
N_EXPERTS = N_EXPERT_GROUPS * EXPERTS_PER_GROUP
TOP_K_IN_GROUP = 2
EXPERT_FF = D_MODEL // 4
NORM_EPS = 1e-6
F32 = jnp.float32

kernel_name = "gated_hybrid_conv_dilated_pool_hmoe_step"


def rmsnorm(x, g):
    xf = x.astype(F32)
    y = xf * lax.rsqrt(jnp.mean(xf * xf, axis=-1, keepdims=True) + NORM_EPS) * g.astype(F32)
    return y.astype(x.dtype)


def layernorm(x, g, b):
    xf = x.astype(F32)
    mu = jnp.mean(xf, axis=-1, keepdims=True)
    xc = xf - mu
    var = jnp.mean(xc * xc, axis=-1, keepdims=True)
    return (xc * lax.rsqrt(var + NORM_EPS) * g.astype(F32) + b.astype(F32)).astype(x.dtype)


def rope(x, pos):
    half = HEAD_DIM // 2
    inv = ROPE_THETA ** (-jnp.arange(half, dtype=F32) / half)
    ang = pos.astype(F32)[:, None] * inv[None, :]
    cos = jnp.cos(ang)[None, :, None, :]
    sin = jnp.sin(ang)[None, :, None, :]
    xf = x.astype(F32)
    x1, x2 = xf[..., :half], xf[..., half:]
    return jnp.concatenate([x1 * cos - x2 * sin, x2 * cos + x1 * sin], axis=-1).astype(x.dtype)


def causal_depthwise_conv(u_ext, w):
    c = u_ext.shape[-1]
    return lax.conv_general_dilated(u_ext, w[:, None, :].astype(u_ext.dtype), window_strides=(1,),
                                    padding="VALID", dimension_numbers=("NWC", "WIO", "NWC"),
                                    feature_group_count=c)


def multiscale_pool(c_ext, pos):
    b, pt, _ = c_ext.shape
    t = pos.shape[0]
    p = pt - t
    cf = c_ext.astype(F32).reshape(b, pt, N_POOL_GROUPS, POOL_GROUP)
    cs = jnp.concatenate([jnp.zeros_like(cf[:, :1]), jnp.cumsum(cf, axis=1)], axis=1)
    outs = []
    for g, w in enumerate(POOL_WINDOWS):
        win_sum = cs[:, p + 1:p + 1 + t, g] - cs[:, p + 1 - w:p + 1 - w + t, g]
        cnt = jnp.minimum(w, pos + 1).astype(F32)
        outs.append(win_sum / cnt[None, :, None] - cf[:, p:, g])
    return jnp.stack(outs, axis=2).astype(c_ext.dtype)


def dilated_attn_prompt(q, k, v, window, dil):
    b, s, h, e = q.shape
    nk = window // dil
    lsub = -(-s // dil)
    nb = -(-lsub // nk)
    lp = nb * nk
    spad = lp * dil

    def to_sub(a):
        a = jnp.pad(a, ((0, 0), (0, spad - s), (0, 0), (0, 0)))
        a = a.reshape(b, lp, dil, h, e).transpose(0, 2, 1, 3, 4)
        return a.reshape(b, dil, nb, nk, h, e)

    qs, ks, vs = to_sub(q), to_sub(k), to_sub(v)

    def with_prev(a):
        prev = jnp.pad(a, ((0, 0), (0, 0), (1, 0), (0, 0), (0, 0), (0, 0)))[:, :, :nb]
        return jnp.concatenate([prev, a], axis=3)

    kb, vb = with_prev(ks), with_prev(vs)
    sc = jnp.einsum("brnqhe,brnkhe->brnhqk", qs, kb, preferred_element_type=F32) * (e ** -0.5)
    qi = jnp.arange(nk)[:, None]
    kk = jnp.arange(2 * nk)[None, :]
    m = nk + qi - kk
    blk = jnp.arange(nb)[:, None, None]
    valid = (m >= 0) & (m <= nk) & ((blk - 1) * nk + kk >= 0)
    sc = jnp.where(valid[None, None, :, None], sc, -jnp.inf)
    lse = jax.nn.logsumexp(sc, axis=-1)
    pr = jnp.exp(sc - lse[..., None])
    o = jnp.einsum("brnhqk,brnkhe->brnqhe", pr.astype(vb.dtype), vb, preferred_element_type=F32)
    o = o.reshape(b, dil, lp, h, e).transpose(0, 2, 1, 3, 4).reshape(b, spad, h, e)[:, :s]
    lse = lse.transpose(0, 1, 2, 4, 3).reshape(b, dil, lp, h).transpose(0, 2, 1, 3).reshape(b, spad, h)[:, :s]
    return o, lse


def dilated_attn_sample(q, k_all, v_all, window, dil, past):
    t = q.shape[1]
    nk = window // dil
    idx = past + jnp.arange(t)[:, None] - dil * jnp.arange(nk + 1)[None, :]
    valid = idx >= 0
    idx = jnp.maximum(idx, 0)
    kg = k_all[:, idx]
    vg = v_all[:, idx]
    sc = jnp.einsum("bthe,btkhe->bthk", q, kg, preferred_element_type=F32) * (q.shape[-1] ** -0.5)
    sc = jnp.where(valid[None, :, None, :], sc, -jnp.inf)
    lse = jax.nn.logsumexp(sc, axis=-1)
    pr = jnp.exp(sc - lse[..., None])
    o = jnp.einsum("bthk,btkhe->bthe", pr.astype(vg.dtype), vg, preferred_element_type=F32)
    return o, lse


def mixer_layer(x, pos, conv_prev, pool_prev, kv_prev, w_in_l, conv_w_l, conv_b_l, ln_g_l, ln_b_l,
                w_a_out_l, w_b_out_l, w_c_map_l, c_scale_l, w_o_l, norm1_l):
    b, t, _ = x.shape
    dt = x.dtype
    h = rmsnorm(x, norm1_l)
    z = jnp.einsum("btd,dn->btn", h, w_in_l)
    sizes = (CONV_CH, CONV_CH, ATTN_WIDTH, ATTN_WIDTH, ATTN_WIDTH, POOL_WIDTH)
    offs = []
    acc = 0
    for sz in sizes:
        acc += sz
        offs.append(acc)
    za1, za2, zq, zk, zv, zc, zg = jnp.split(z, offs, axis=-1)

    u = za1 * jax.nn.sigmoid(za2)
    u_ext = jnp.concatenate([conv_prev.astype(dt), u], axis=1)
    a = causal_depthwise_conv(u_ext, conv_w_l) + conv_b_l.astype(dt)
    a = jax.nn.silu(layernorm(a, ln_g_l, ln_b_l))
    a = jnp.einsum("btc,cd->btd", a, w_a_out_l)

    q = rope(zq.reshape(b, t, N_ATTN_HEADS, HEAD_DIM), pos)
    k = rope(zk.reshape(b, t, N_ATTN_HEADS, HEAD_DIM), pos)
    v = zv.reshape(b, t, N_ATTN_HEADS, HEAD_DIM)
    outs, lses, new_kv = [], [], []
    for g, (win, dil) in enumerate(ATTN_GROUPS):
        hs = slice(g * HEADS_PER_GROUP, (g + 1) * HEADS_PER_GROUP)
        qg, kg, vg = q[:, :, hs], k[:, :, hs], v[:, :, hs]
        if kv_prev is None:
            o, lse = dilated_attn_prompt(qg, kg, vg, win, dil)
            keep = min(win, t)
            new_kv.append(jnp.stack([kg[:, t - keep:], vg[:, t - keep:]], axis=2))
        else:
            prev = kv_prev[g].astype(dt)
            past = prev.shape[1]
            k_all = jnp.concatenate([prev[:, :, 0], kg], axis=1)
            v_all = jnp.concatenate([prev[:, :, 1], vg], axis=1)
            o, lse = dilated_attn_sample(qg, k_all, v_all, win, dil, past)
            new_kv.append(jnp.stack([k_all[:, -past:], v_all[:, -past:]], axis=2))
        outs.append(o)
        lses.append(lse)
    wts = jax.nn.softmax(jnp.stack(lses, axis=0), axis=0)
    o = jnp.sum(wts[..., None] * jnp.stack(outs, axis=0), axis=0)
    bb = jnp.einsum("btc,cd->btd", o.reshape(b, t, ATTN_OUT_WIDTH).astype(dt), w_b_out_l)

    c_ext = jnp.concatenate([pool_prev.astype(dt), zc], axis=1)
    cp = multiscale_pool(c_ext, pos)
    cb = jnp.einsum("btgc,gce->btge", cp, w_c_map_l).reshape(b, t, D_MODEL) * c_scale_l.astype(dt)

    gates = jax.nn.sigmoid(zg.reshape(b, t, N_BRANCHES, D_MODEL))
    mrg = gates[:, :, 0] * a + gates[:, :, 1] * bb + gates[:, :, 2] * cb
    y = x + jnp.einsum("btd,de->bte", mrg, w_o_l)
    return y, u_ext[:, -(DW_CONV_LEN - 1):], c_ext[:, -(MAX_POOL - 1):], new_kv


def hier_moe(h, w_rg, b_rg, w_re, b_re, w_gate, w_up, w_down):
    b, t, _ = h.shape
    hf = h.astype(F32)
    lg = hf @ w_rg.astype(F32) + b_rg.astype(F32)
    pg = jax.nn.softmax(lg, axis=-1)
    g_idx = jnp.argmax(lg, axis=-1)
    g_sel = jax.nn.one_hot(g_idx, N_EXPERT_GROUPS, dtype=F32)
    p_group = jnp.sum(pg * g_sel, axis=-1)
    le = (hf @ w_re.astype(F32) + b_re.astype(F32)).reshape(b, t, N_EXPERT_GROUPS, EXPERTS_PER_GROUP)
    le_sel = jnp.einsum("btge,btg->bte", le, g_sel)
    top_v, top_i = lax.top_k(le_sel, TOP_K_IN_GROUP)
    wk = jax.nn.softmax(top_v, axis=-1) * p_group[..., None]
    e_idx = g_idx[..., None] * EXPERTS_PER_GROUP + top_i
    gate = jnp.einsum("btk,btke->bte", wk, jax.nn.one_hot(e_idx, N_EXPERTS, dtype=F32))
    hg = jnp.einsum("btd,edf->btef", h, w_gate)
    hu = jnp.einsum("btd,edf->btef", h, w_up)
    act = jax.nn.silu(hg) * hu * gate[..., None].astype(h.dtype)
    return jnp.einsum("btef,efd->btd", act, w_down)


def setup_inputs(seed: int = 0) -> dict:
    key = jax.random.key(seed)
    ks = jax.random.split(key, 32)
    D = D_MODEL

    def nrm(k, shape, scale=1.0):
        return jax.random.normal(k, shape, F32) * scale

    kv_len = [min(w, PAST_LEN) for (w, _) in ATTN_GROUPS]
    return {
        "x_prompt": nrm(ks[0], (BATCH, SEQ, D)),
        "x_sample": nrm(ks[1], (DEC_BATCH, DEC_SEQ, D)),
        "cache_kv_w128": nrm(ks[2], (DEPTH, DEC_BATCH, kv_len[0], 2, HEADS_PER_GROUP, HEAD_DIM)),
        "cache_kv_w512": nrm(ks[3], (DEPTH, DEC_BATCH, kv_len[1], 2, HEADS_PER_GROUP, HEAD_DIM)),
        "cache_kv_w2048": nrm(ks[4], (DEPTH, DEC_BATCH, kv_len[2], 2, HEADS_PER_GROUP, HEAD_DIM)),
        "state_conv": nrm(ks[5], (DEPTH, DEC_BATCH, DW_CONV_LEN - 1, CONV_CH), 0.5),
        "state_pool": nrm(ks[6], (DEPTH, DEC_BATCH, MAX_POOL - 1, POOL_WIDTH)),
        "norm1": 1.0 + nrm(ks[7], (DEPTH, D), 0.02),
        "w_in": nrm(ks[8], (DEPTH, D, IN_WIDTH), D ** -0.5),
        "conv_w": nrm(ks[9], (DEPTH, DW_CONV_LEN, CONV_CH), DW_CONV_LEN ** -0.5),
        "conv_b": nrm(ks[10], (DEPTH, CONV_CH), 0.02),
        "ln_g": 1.0 + nrm(ks[11], (DEPTH, CONV_CH), 0.02),
        "ln_b": nrm(ks[12], (DEPTH, CONV_CH), 0.02),
        "w_a_out": nrm(ks[13], (DEPTH, CONV_CH, D), CONV_CH ** -0.5),
        "w_b_out": nrm(ks[14], (DEPTH, ATTN_OUT_WIDTH, D), ATTN_OUT_WIDTH ** -0.5),
        "w_c_map": nrm(ks[15], (DEPTH, N_POOL_GROUPS, POOL_GROUP, POOL_OUT_GROUP), POOL_GROUP ** -0.5),
        "c_scale": 1.0 + nrm(ks[16], (DEPTH, D), 0.02),
        "w_o": nrm(ks[17], (DEPTH, D, D), D ** -0.5),
        "norm2": 1.0 + nrm(ks[18], (DEPTH, D), 0.02),
        "w_router_group": nrm(ks[19], (DEPTH, D, N_EXPERT_GROUPS), D ** -0.5),
        "b_router_group": nrm(ks[20], (DEPTH, N_EXPERT_GROUPS), 0.01),
        "w_router_expert": nrm(ks[21], (DEPTH, D, N_EXPERTS), D ** -0.5),
        "b_router_expert": nrm(ks[22], (DEPTH, N_EXPERTS), 0.01),
        "w_gate": nrm(ks[23], (DEPTH, N_EXPERTS, D, EXPERT_FF), D ** -0.5),
        "w_up": nrm(ks[24], (DEPTH, N_EXPERTS, D, EXPERT_FF), D ** -0.5),
        "w_down": nrm(ks[25], (DEPTH, N_EXPERTS, EXPERT_FF, D), EXPERT_FF ** -0.5),
        "norm_final": 1.0 + nrm(ks[26], (D,), 0.02),
    }


def reference(x_prompt, x_sample, cache_kv_w128, cache_kv_w512, cache_kv_w2048, state_conv, state_pool,
              norm1, w_in, conv_w, conv_b, ln_g, ln_b, w_a_out, w_b_out, w_c_map, c_scale, w_o,
              norm2, w_router_group, b_router_group, w_router_expert, b_router_expert,
              w_gate, w_up, w_down, norm_final):
    pos_p = jnp.arange(x_prompt.shape[1], dtype=jnp.int32)
    pos_s = PAST_LEN + jnp.arange(x_sample.shape[1], dtype=jnp.int32)
    caches = (cache_kv_w128, cache_kv_w512, cache_kv_w2048)
    xp, xs = x_prompt, x_sample
    pc, pp, pkv = [], [], [[] for _ in ATTN_GROUPS]
    sc, sp, skv = [], [], [[] for _ in ATTN_GROUPS]
    for l in range(DEPTH):
        lw = (w_in[l], conv_w[l], conv_b[l], ln_g[l], ln_b[l], w_a_out[l], w_b_out[l], w_c_map[l],
              c_scale[l], w_o[l], norm1[l])
        mw = (w_router_group[l], b_router_group[l], w_router_expert[l], b_router_expert[l],
              w_gate[l], w_up[l], w_down[l])
        zconv = jnp.zeros((xp.shape[0], DW_CONV_LEN - 1, CONV_CH), xp.dtype)
        zpool = jnp.zeros((xp.shape[0], MAX_POOL - 1, POOL_WIDTH), xp.dtype)
        xp, c1, p1, kv1 = mixer_layer(xp, pos_p, zconv, zpool, None, *lw)
        xs, c2, p2, kv2 = mixer_layer(xs, pos_s, state_conv[l], state_pool[l], [c[l] for c in caches], *lw)
        xp = xp + hier_moe(rmsnorm(xp, norm2[l]), *mw)
        xs = xs + hier_moe(rmsnorm(xs, norm2[l]), *mw)
        pc.append(c1)
        pp.append(p1)
        sc.append(c2)
        sp.append(p2)
        for g in range(len(ATTN_GROUPS)):
            pkv[g].append(kv1[g])
            skv[g].append(kv2[g])
    y_prompt = rmsnorm(xp, norm_final)
    y_sample = rmsnorm(xs, norm_final)
    new_conv_prompt = jnp.stack(pc, axis=0)
    new_pool_prompt = jnp.stack(pp, axis=0)
    new_kv128_prompt = jnp.stack(pkv[0], axis=0)
    new_kv512_prompt = jnp.stack(pkv[1], axis=0)
    new_kv2048_prompt = jnp.stack(pkv[2], axis=0)
    new_conv_sample = jnp.stack(sc, axis=0)
    new_pool_sample = jnp.stack(sp, axis=0)
    new_kv128_sample = jnp.stack(skv[0], axis=0)
    new_kv512_sample = jnp.stack(skv[1], axis=0)
    new_kv2048_sample = jnp.stack(skv[2], axis=0)
    return (y_prompt, y_sample, new_conv_prompt, new_pool_prompt, new_kv128_prompt, new_kv512_prompt,
            new_kv2048_prompt, new_conv_sample, new_pool_sample, new_kv128_sample, new_kv512_sample,
            new_kv2048_sample)
```

```python
import functools

import jax
import jax.numpy as jnp
from jax import lax
from jax.experimental import pallas as pl
from jax.experimental.pallas import tpu as pltpu

F32 = jnp.float32
BF16 = jnp.bfloat16

D_MODEL = 2048
HEAD_DIM = 128
HEADS_PER_GROUP = 4
ATTN_GROUPS = ((128, 1), (512, 4), (2048, 16))
N_GROUPS = len(ATTN_GROUPS)
ATTN_WIDTH = HEADS_PER_GROUP * N_GROUPS * HEAD_DIM
ATTN_OUT_WIDTH = HEADS_PER_GROUP * HEAD_DIM
ROPE_THETA = 10000.0
CONV_CH = D_MODEL // 2
DW_CONV_LEN = 31
CONV_HALO = DW_CONV_LEN - 1
POOL_WINDOWS = (2, 4, 8, 16)
POOL_GROUP = D_MODEL // 8
POOL_WIDTH = len(POOL_WINDOWS) * POOL_GROUP
POOL_OUT_GROUP = D_MODEL // len(POOL_WINDOWS)
POOL_HALO = max(POOL_WINDOWS) - 1
N_BRANCHES = 3
N_EXPERT_GROUPS = 4
EXPERTS_PER_GROUP = 4
N_EXPERTS = N_EXPERT_GROUPS * EXPERTS_PER_GROUP
EXPERT_FF = D_MODEL // 4
NORM_EPS = 1e-6
PAST_LEN = 8192
NK = 128

V7X_VMEM_BYTES = 64 * 1024 * 1024
LANES = 128
ROUTER_LANES = LANES
EXT_WIDTH = D_MODEL + ROUTER_LANES
IN_TILE_N = 512
NEG_INF = float("-inf")


def _vmem_limit(nbytes):
    return int(min(V7X_VMEM_BYTES - (6 << 20), max(nbytes, 16 << 20)))


def _rms(x, g):
    return x * lax.rsqrt(jnp.mean(x * x, axis=-1, keepdims=True) + NORM_EPS) * g


def _sigmoid(x):
    return 1.0 / (1.0 + jnp.exp(-x))


def _rope_tables(pos):
    half = HEAD_DIM // 2
    inv = ROPE_THETA ** (-jnp.arange(half, dtype=F32) / half)
    ang = pos.astype(F32)[:, None] * inv[None, :]
    c, s = jnp.cos(ang), jnp.sin(ang)
    return jnp.concatenate([c, c], axis=-1), jnp.concatenate([-s, s], axis=-1)


def _rope(x, cos2, sin2):
    return x * cos2 + pltpu.roll(x, HEAD_DIM // 2, x.ndim - 1) * sin2


_IN_SPLIT = (2 * CONV_CH, 3 * ATTN_WIDTH, POOL_WIDTH, N_BRANCHES * D_MODEL)


def _in_proj_kernel(x_ref, g_ref, w_ref, za_ref, zq_ref, zc_ref, zg_ref, h_ref, *, bounds):
    j = pl.program_id(1)

    @pl.when(j == 0)
    def _():
        h_ref[...] = _rms(x_ref[...], g_ref[...]).astype(BF16)

    z = jnp.dot(h_ref[...], w_ref[...].astype(BF16), preferred_element_type=F32)
    for (lo, hi), ref in zip(bounds, (za_ref, zq_ref, zc_ref, zg_ref)):
        @pl.when((j >= lo) & (j < hi))
        def _(ref=ref):
            ref[...] = z


def _in_proj(x, norm_g, w_in, layer, *, tm):
    m, d = x.shape
    n = w_in.shape[-1]
    tn = IN_TILE_N
    ntiles = [wd // tn for wd in _IN_SPLIT]
    bounds, lo = [], 0
    for t in ntiles:
        bounds.append((lo, lo + t))
        lo += t

    def omap(lo, cnt):
        return lambda i, j: (i, jnp.clip(j - lo, 0, cnt - 1))

    est = tm * d * 4 + tm * d * 2 + 2 * d * tn * 4 + d * tn * 2 + 4 * 2 * tm * tn * 4 + tm * d * 4
    return pl.pallas_call(
        functools.partial(_in_proj_kernel, bounds=tuple(bounds)),
        grid=(m // tm, n // tn),
        in_specs=[
            pl.BlockSpec((tm, d), lambda i, j: (i, 0), pipeline_mode=pl.Buffered(1)),
            pl.BlockSpec((None, 1, d), lambda i, j: (layer, 0, 0)),
            pl.BlockSpec((None, d, tn), lambda i, j: (layer, 0, j)),
        ],
        out_specs=[pl.BlockSpec((tm, tn), omap(b[0], t)) for b, t in zip(bounds, ntiles)],
        out_shape=[jax.ShapeDtypeStruct((m, wd), F32) for wd in _IN_SPLIT],
        scratch_shapes=[pltpu.VMEM((tm, d), BF16)],
        compiler_params=pltpu.CompilerParams(
            dimension_semantics=("parallel", "arbitrary"), vmem_limit_bytes=_vmem_limit(est + (8 << 20))),
        name="in_proj",
    )(x, norm_g.reshape(norm_g.shape[0], 1, d), w_in)


CONV_PAD = 32
POOL_PAD = 16


def _conv_ln_silu(ubuf, abuf, w_ref, b_ref, lg_ref, lb_ref, t_rows, row_chunk, col_chunk):
    off = CONV_PAD - CONV_HALO
    for r0 in range(0, t_rows, row_chunk):
        for c0 in range(0, CONV_CH, col_chunk):
            acc = jnp.zeros((row_chunk, col_chunk), F32) + b_ref[:, c0:c0 + col_chunk]
            for k in range(DW_CONV_LEN):
                acc = acc + w_ref[k:k + 1, c0:c0 + col_chunk] * ubuf[r0 + off + k:r0 + off + k + row_chunk,
                                                                    c0:c0 + col_chunk]
            abuf[r0:r0 + row_chunk, c0:c0 + col_chunk] = acc
    a = abuf[...]
    mu = jnp.mean(a, axis=-1, keepdims=True)
    xc = a - mu
    var = jnp.mean(xc * xc, axis=-1, keepdims=True)
    y = xc * lax.rsqrt(var + NORM_EPS) * lg_ref[...] + lb_ref[...]
    return y * _sigmoid(y)


def _branch_ac_prompt_kernel(za1_ref, za2_ref, zc_ref, w_ref, b_ref, lg_ref, lb_ref,
                             a_ref, cp_ref, nconv_ref, npool_ref, ubuf, abuf, cbuf, *, t_rows):
    t = pl.program_id(1)
    nt = pl.num_programs(1)

    @pl.when(t == 0)
    def _():
        ubuf[0:CONV_PAD, :] = jnp.zeros((CONV_PAD, CONV_CH), F32)
        cbuf[0:POOL_PAD, :] = jnp.zeros((POOL_PAD, POOL_WIDTH), F32)

    ubuf[CONV_PAD:CONV_PAD + t_rows, :] = za1_ref[...] * _sigmoid(za2_ref[...])
    a_ref[...] = _conv_ln_silu(ubuf, abuf, w_ref, b_ref, lg_ref, lb_ref, t_rows, 64, 512).astype(BF16)

    cbuf[POOL_PAD:POOL_PAD + t_rows, :] = zc_ref[...]
    rc = 128
    for r0 in range(0, t_rows, rc):
        pos = t * t_rows + r0 + lax.broadcasted_iota(jnp.int32, (rc, 1), 0)
        for g, w in enumerate(POOL_WINDOWS):
            cs = slice(g * POOL_GROUP, (g + 1) * POOL_GROUP)
            s = cbuf[POOL_PAD + r0:POOL_PAD + r0 + rc, cs]
            cur = s
            for i in range(1, w):
                s = s + cbuf[POOL_PAD + r0 - i:POOL_PAD + r0 - i + rc, cs]
            cnt = jnp.minimum(w, pos + 1).astype(F32)
            cp_ref[r0:r0 + rc, cs] = (s / cnt - cur).astype(BF16)

    @pl.when(t == nt - 1)
    def _():
        nconv_ref[...] = ubuf[CONV_PAD + t_rows - CONV_HALO:CONV_PAD + t_rows, :]
        npool_ref[...] = cbuf[POOL_PAD + t_rows - POOL_HALO:POOL_PAD + t_rows, :]

    ubuf[0:CONV_PAD, :] = ubuf[t_rows:t_rows + CONV_PAD, :]
    cbuf[0:POOL_PAD, :] = cbuf[t_rows:t_rows + POOL_PAD, :]


def _branch_ac_prompt(za, zc, conv_w, conv_b, ln_g, ln_b, layer, *, nb, seq, t_rows=256):
    m = nb * seq
    nt = seq // t_rows
    vec = lambda a: a.reshape(a.shape[0], 1, a.shape[-1])
    vspec = pl.BlockSpec((None, 1, CONV_CH), lambda b, t: (layer, 0, 0))
    return pl.pallas_call(
        functools.partial(_branch_ac_prompt_kernel, t_rows=t_rows),
        grid=(nb, nt),
        in_specs=[
            pl.BlockSpec((t_rows, CONV_CH), lambda b, t: (b * nt + t, 0)),
            pl.BlockSpec((t_rows, CONV_CH), lambda b, t: (b * nt + t, 1)),
            pl.BlockSpec((t_rows, POOL_WIDTH), lambda b, t: (b * nt + t, 0)),
            pl.BlockSpec((None, DW_CONV_LEN, CONV_CH), lambda b, t: (layer, 0, 0)),
            vspec, vspec, vspec,
        ],
        out_specs=[
            pl.BlockSpec((t_rows, CONV_CH), lambda b, t: (b * nt + t, 0)),
            pl.BlockSpec((t_rows, POOL_WIDTH), lambda b, t: (b * nt + t, 0)),
            pl.BlockSpec((None, CONV_HALO, CONV_CH), lambda b, t: (b, 0, 0)),
            pl.BlockSpec((None, POOL_HALO, POOL_WIDTH), lambda b, t: (b, 0, 0)),
        ],
        out_shape=[
            jax.ShapeDtypeStruct((m, CONV_CH), BF16),
            jax.ShapeDtypeStruct((m, POOL_WIDTH), BF16),
            jax.ShapeDtypeStruct((nb, CONV_HALO, CONV_CH), F32),
            jax.ShapeDtypeStruct((nb, POOL_HALO, POOL_WIDTH), F32),
        ],
        scratch_shapes=[
            pltpu.VMEM((CONV_PAD + t_rows, CONV_CH), F32),
            pltpu.VMEM((t_rows, CONV_CH), F32),
            pltpu.VMEM((POOL_PAD + t_rows, POOL_WIDTH), F32),
        ],
        compiler_params=pltpu.CompilerParams(
            dimension_semantics=("parallel", "arbitrary"), vmem_limit_bytes=_vmem_limit(32 << 20)),
        name="branch_ac_prompt",
    )(za, za, zc, conv_w, vec(conv_b), vec(ln_g), vec(ln_b))


def _branch_ac_sample_kernel(za1_ref, za2_ref, zc_ref, sconv_ref, spool_ref, w_ref, b_ref, lg_ref, lb_ref,
                             a_ref, cp_ref, nconv_ref, npool_ref, uext, cext, *, nb, steps, past):
    for i in range(CONV_HALO):
        uext[i] = sconv_ref[i]
    for j in range(steps):
        rows = slice(j * nb, (j + 1) * nb)
        uext[CONV_HALO + j] = za1_ref[rows, :] * _sigmoid(za2_ref[rows, :])
    for j in range(steps):
        acc = jnp.zeros((nb, CONV_CH), F32) + b_ref[...]
        for k in range(DW_CONV_LEN):
            acc = acc + w_ref[k:k + 1, :] * uext[j + k]
        mu = jnp.mean(acc, axis=-1, keepdims=True)
        xc = acc - mu
        var = jnp.mean(xc * xc, axis=-1, keepdims=True)
        y = xc * lax.rsqrt(var + NORM_EPS) * lg_ref[...] + lb_ref[...]
        a_ref[j * nb:(j + 1) * nb, :] = (y * _sigmoid(y)).astype(BF16)
    for i in range(CONV_HALO):
        nconv_ref[i] = uext[steps + i]

    for i in range(POOL_HALO):
        cext[i] = spool_ref[i]
    for j in range(steps):
        cext[POOL_HALO + j] = zc_ref[j * nb:(j + 1) * nb, :]
    for j in range(steps):
        for g, w in enumerate(POOL_WINDOWS):
            cs = slice(g * POOL_GROUP, (g + 1) * POOL_GROUP)
            cur = cext[POOL_HALO + j, :, cs]
            s = cur
            for i in range(1, w):
                s = s + cext[POOL_HALO + j - i, :, cs]
            cnt = float(min(w, past + j + 1))
            cp_ref[j * nb:(j + 1) * nb, cs] = (s / cnt - cur).astype(BF16)
    for i in range(POOL_HALO):
        npool_ref[i] = cext[steps + i]


def _branch_ac_sample(za, zc, sconv_tm, spool_tm, conv_w, conv_b, ln_g, ln_b, layer, *, nb, steps, past):
    m = nb * steps
    vec = lambda a: a.reshape(a.shape[0], 1, a.shape[-1])
    vspec = pl.BlockSpec((None, 1, CONV_CH), lambda i: (layer, 0, 0))
    return pl.pallas_call(
        functools.partial(_branch_ac_sample_kernel, nb=nb, steps=steps, past=past),
        grid=(1,),
        in_specs=[
            pl.BlockSpec((m, CONV_CH), lambda i: (0, 0)),
            pl.BlockSpec((m, CONV_CH), lambda i: (0, 1)),
            pl.BlockSpec((m, POOL_WIDTH), lambda i: (0, 0)),
            pl.BlockSpec((None, CONV_HALO, nb, CONV_CH), lambda i: (layer, 0, 0, 0)),
            pl.BlockSpec((None, POOL_HALO, nb, POOL_WIDTH), lambda i: (layer, 0, 0, 0)),
            pl.BlockSpec((None, DW_CONV_LEN, CONV_CH), lambda i: (layer, 0, 0)),
            vspec, vspec, vspec,
        ],
        out_specs=[
            pl.BlockSpec((m, CONV_CH), lambda i: (0, 0)),
            pl.BlockSpec((m, POOL_WIDTH), lambda i: (0, 0)),
            pl.BlockSpec((CONV_HALO, nb, CONV_CH), lambda i: (0, 0, 0)),
            pl.BlockSpec((POOL_HALO, nb, POOL_WIDTH), lambda i: (0, 0, 0)),
        ],
        out_shape=[
            jax.ShapeDtypeStruct((m, CONV_CH), BF16),
            jax.ShapeDtypeStruct((m, POOL_WIDTH), BF16),
            jax.ShapeDtypeStruct((CONV_HALO, nb, CONV_CH), F32),
            jax.ShapeDtypeStruct((POOL_HALO, nb, POOL_WIDTH), F32),
        ],
        scratch_shapes=[
            pltpu.VMEM((CONV_HALO + steps, nb, CONV_CH), F32),
            pltpu.VMEM((POOL_HALO + steps, nb, POOL_WIDTH), F32),
        ],
        compiler_params=pltpu.CompilerParams(vmem_limit_bytes=_vmem_limit(48 << 20)),
        name="branch_ac_sample",
    )(za, za, zc, sconv_tm, spool_tm, conv_w, vec(conv_b), vec(ln_g), vec(ln_b))


def _attn_prompt_kernel(*refs, seq):
    qkv_refs = refs[0:9]
    cos_ref, sin_ref = refs[9:11]
    o_ref = refs[11]
    ko_refs = refs[12:15]
    qs, og, lse = refs[15:18]
    scale = HEAD_DIM ** -0.5

    cos2 = cos_ref[...]
    sin2 = sin_ref[...]
    for g in range(N_GROUPS):
        qs[g] = _rope(qkv_refs[3 * g][...], cos2, sin2)
        ko_refs[g][...] = _rope(qkv_refs[3 * g + 1][...], cos2, sin2)

    def rows(start, count, dil):
        if dil == 1:
            return pl.ds(start if isinstance(start, int) else pl.multiple_of(start, NK), count)
        return pl.ds(start, count, stride=dil)

    def block(g, dil, q_start, k_start, nkeys):
        k_ref, v_ref = ko_refs[g], qkv_refs[3 * g + 2]
        q = qs[g, rows(q_start, NK, dil), :].astype(BF16)
        k = k_ref[rows(k_start, nkeys, dil), :].astype(BF16)
        v = v_ref[rows(k_start, nkeys, dil), :].astype(BF16)
        s = lax.dot_general(q, k, (((1,), (1,)), ((), ())), preferred_element_type=F32) * scale
        qi = lax.broadcasted_iota(jnp.int32, (NK, nkeys), 0)
        kk = lax.broadcasted_iota(jnp.int32, (NK, nkeys), 1)
        rel = qi + (nkeys - NK) - kk
        s = jnp.where((rel >= 0) & (rel <= NK), s, NEG_INF)
        mx = jnp.max(s, axis=-1, keepdims=True)
        p = jnp.exp(s - mx)
        l = jnp.sum(p, axis=-1, keepdims=True)
        acc = jnp.dot(p.astype(BF16), v, preferred_element_type=F32)
        og[g, rows(q_start, NK, dil), :] = acc / l
        lse[g, rows(q_start, NK, dil), :] = jnp.broadcast_to(mx + jnp.log(l), (NK, HEAD_DIM))

    for g, (win, dil) in enumerate(ATTN_GROUPS):
        span = NK * dil
        nblk = seq // span

        def per_residue(r, carry, g=g, dil=dil, span=span, nblk=nblk):
            block(g, dil, r, r, NK)

            def per_block(n, c):
                block(g, dil, n * span + r, (n - 1) * span + r, 2 * NK)
                return c

            if nblk > 1:
                lax.fori_loop(1, nblk, per_block, 0)
            return carry

        if dil == 1:
            per_residue(0, 0)
        else:
            lax.fori_loop(0, dil, per_residue, 0)

    m = jnp.maximum(jnp.maximum(lse[0], lse[1]), lse[2])
    num = jnp.zeros((seq, HEAD_DIM), F32)
    den = jnp.zeros((seq, HEAD_DIM), F32)
    for g in range(N_GROUPS):
        w = jnp.exp(lse[g] - m)
        num = num + w * og[g]
        den = den + w
    o_ref[...] = (num / den).astype(BF16)


def _attn_prompt(zq, cos2, sin2, *, nb, seq):
    m = nb * seq
    for win, dil in ATTN_GROUPS:
        assert win // dil == NK and seq % (NK * dil) == 0
    nh = HEADS_PER_GROUP * N_GROUPS

    def col(kind, g):
        return lambda b, j: (b, kind * nh + g * HEADS_PER_GROUP + j)

    in_specs = []
    for g in range(N_GROUPS):
        for kind in range(3):
            in_specs.append(pl.BlockSpec((seq, HEAD_DIM), col(kind, g)))
    in_specs += [pl.BlockSpec((seq, HEAD_DIM), lambda b, j: (0, 0))] * 2
    blk = seq * HEAD_DIM * 4
    est = 2 * 9 * blk + 4 * blk + 2 * blk // 2 + 2 * 3 * blk + 9 * blk
    return pl.pallas_call(
        functools.partial(_attn_prompt_kernel, seq=seq),
        grid=(nb, HEADS_PER_GROUP),
        in_specs=in_specs,
        out_specs=[pl.BlockSpec((seq, HEAD_DIM), lambda b, j: (b, j))] * 4,
        out_shape=[jax.ShapeDtypeStruct((m, ATTN_OUT_WIDTH), BF16)]
        + [jax.ShapeDtypeStruct((m, ATTN_OUT_WIDTH), F32)] * 3,
        scratch_shapes=[pltpu.VMEM((N_GROUPS, seq, HEAD_DIM), F32)] * 3,
        compiler_params=pltpu.CompilerParams(
            dimension_semantics=("parallel", "arbitrary"), vmem_limit_bytes=_vmem_limit(est + (8 << 20))),
        name="attn_prompt",
    )(*([zq] * 9), cos2, sin2)


def _attn_sample_kernel(qkv_ref, cos_ref, sin_ref, c0_ref, c1_ref, c2_ref, o_ref, kn_ref, *, steps):
    scale = HEAD_DIM ** -0.5
    ng = N_GROUPS
    q = [[None] * ng for _ in range(steps)]
    kn = [[None] * ng for _ in range(steps)]
    for j in range(steps):
        cos2 = cos_ref[j:j + 1, :]
        sin2 = sin_ref[j:j + 1, :]
        for g in range(ng):
            q[j][g] = _rope(qkv_ref[j, g], cos2, sin2)
            kn[j][g] = _rope(qkv_ref[j, ng + g], cos2, sin2)
            kn_ref[j, g] = kn[j][g]

    for j in range(steps):
        parts = []
        for g, (win, dil) in enumerate(ATTN_GROUPS):
            if g == 0:
                kc, vc = c0_ref[:, 0], c0_ref[:, 1]
            elif g == 1:
                kc, vc = c1_ref[:, j, 0], c1_ref[:, j, 1]
            else:
                kc, vc = c2_ref[:, j, 0], c2_ref[:, j, 1]
            qv = q[j][g]
            s = jnp.sum(kc * qv[None], axis=-1, keepdims=True) * scale
            if dil == 1:
                ri = lax.broadcasted_iota(jnp.int32, s.shape, 0)
                s = jnp.where(ri >= j, s, NEG_INF)
                new_js = list(range(j + 1))
            else:
                new_js = [j]
            s_new = [jnp.sum(kn[jn][g] * qv, axis=-1, keepdims=True) * scale for jn in new_js]
            mx = jnp.max(s, axis=0)
            for sn in s_new:
                mx = jnp.maximum(mx, sn)
            p = jnp.exp(s - mx[None])
            l = jnp.sum(p, axis=0)
            acc = jnp.sum(p * vc, axis=0)
            for jn, sn in zip(new_js, s_new):
                pn = jnp.exp(sn - mx)
                l = l + pn
                acc = acc + pn * qkv_ref[jn, 2 * ng + g]
            parts.append((acc / l, mx + jnp.log(l)))
        m = jnp.maximum(jnp.maximum(parts[0][1], parts[1][1]), parts[2][1])
        num = jnp.zeros((HEADS_PER_GROUP, HEAD_DIM), F32)
        den = jnp.zeros((HEADS_PER_GROUP, 1), F32)
        for o_g, lse_g in parts:
            w = jnp.exp(lse_g - m)
            num = num + w * o_g
            den = den + w
        o_ref[j] = num / den


def _attn_sample(qkv5, cos2, sin2, c0, c1, c2, layer, *, nb, steps):
    hd = (HEADS_PER_GROUP, HEAD_DIM)
    return pl.pallas_call(
        functools.partial(_attn_sample_kernel, steps=steps),
        grid=(nb,),
        in_specs=[
            pl.BlockSpec((steps, None, 3 * N_GROUPS) + hd, lambda b: (0, b, 0, 0, 0)),
            pl.BlockSpec((steps, HEAD_DIM), lambda b: (0, 0)),
            pl.BlockSpec((steps, HEAD_DIM), lambda b: (0, 0)),
            pl.BlockSpec((None, None, NK, 2) + hd, lambda b: (layer, b, 0, 0, 0, 0)),
            pl.BlockSpec((None, None, NK, steps, 2) + hd, lambda b: (layer, b, 0, 0, 0, 0, 0)),
            pl.BlockSpec((None, None, NK, steps, 2) + hd, lambda b: (layer, b, 0, 0, 0, 0, 0)),
        ],
        out_specs=[
            pl.BlockSpec((steps, None) + hd, lambda b: (0, b, 0, 0)),
            pl.BlockSpec((steps, None, N_GROUPS) + hd, lambda b: (0, b, 0, 0, 0)),
        ],
        out_shape=[
            jax.ShapeDtypeStruct((steps, nb) + hd, F32),
            jax.ShapeDtypeStruct((steps, nb, N_GROUPS) + hd, F32),
        ],
        compiler_params=pltpu.CompilerParams(
            dimension_semantics=("parallel",), vmem_limit_bytes=_vmem_limit(40 << 20)),
        name="attn_sample",
    )(qkv5, cos2, sin2, c0, c1, c2)


def _split_bf16(x):
    hi = x.astype(BF16)
    return hi, (x - hi.astype(F32)).astype(BF16)


def _merge_kernel(x_ref, a_ref, o_ref, cp_ref, g0_ref, g1_ref, g2_ref, wa_ref, wb_ref, wc_ref, cs_ref, wo_ref,
                  n2_ref, wrh_ref, wrl_ref, br_ref, y_ref):
    a = jnp.dot(a_ref[...], wa_ref[...], preferred_element_type=F32)
    bb = jnp.dot(o_ref[...], wb_ref[...], preferred_element_type=F32)
    mrg = _sigmoid(g0_ref[...]) * a + _sigmoid(g1_ref[...]) * bb
    cs = cs_ref[...]
    for g in range(len(POOL_WINDOWS)):
        oc = slice(g * POOL_OUT_GROUP, (g + 1) * POOL_OUT_GROUP)
        cb = jnp.dot(cp_ref[:, g * POOL_GROUP:(g + 1) * POOL_GROUP], wc_ref[g], preferred_element_type=F32)
        y_ref[:, oc] = mrg[:, oc] + _sigmoid(g2_ref[:, oc]) * (cb * cs[:, oc])
    mrg = y_ref[:, 0:D_MODEL].astype(BF16)
    y = x_ref[...] + jnp.dot(mrg, wo_ref[...], preferred_element_type=F32)
    y_ref[:, 0:D_MODEL] = y

    h_hi, h_lo = _split_bf16(_rms(y, n2_ref[...]))
    wrh = wrh_ref[...]
    logits = (jnp.dot(h_hi, wrh, preferred_element_type=F32) + jnp.dot(h_lo, wrh, preferred_element_type=F32)
              + jnp.dot(h_hi, wrl_ref[...], preferred_element_type=F32) + br_ref[...])
    lane = lax.broadcasted_iota(jnp.int32, logits.shape, 1).astype(F32)
    far = float(ROUTER_LANES)
    is_g = lane < N_EXPERT_GROUPS
    gmax = jnp.max(jnp.where(is_g, logits, NEG_INF), axis=-1, keepdims=True)
    g_idx = jnp.min(jnp.where(is_g & (logits == gmax), lane, far), axis=-1, keepdims=True)
    p_group = 1.0 / jnp.sum(jnp.where(is_g, jnp.exp(logits - gmax), 0.0), axis=-1, keepdims=True)
    e_lo = N_EXPERT_GROUPS + g_idx * EXPERTS_PER_GROUP
    sel = (lane >= e_lo) & (lane < e_lo + EXPERTS_PER_GROUP)
    t1 = jnp.max(jnp.where(sel, logits, NEG_INF), axis=-1, keepdims=True)
    i1 = jnp.min(jnp.where(sel & (logits == t1), lane, far), axis=-1, keepdims=True)
    sel2 = sel & (lane != i1)
    t2 = jnp.max(jnp.where(sel2, logits, NEG_INF), axis=-1, keepdims=True)
    i2 = jnp.min(jnp.where(sel2 & (logits == t2), lane, far), axis=-1, keepdims=True)
    r = jnp.exp(t2 - t1)
    w1 = p_group / (1.0 + r)
    w2 = p_group * r / (1.0 + r)
    gate = jnp.where(lane == i1, w1, 0.0) + jnp.where(lane == i2, w2, 0.0)
    y_ref[:, D_MODEL:EXT_WIDTH] = jnp.where(lane == 0.0, g_idx, gate)


def _merge(x, a_act, o, cp, zg, wa, wb, wc, c_scale, wo, norm2, wr_hi, wr_lo, b_r, layer, *, tm):
    m = x.shape[0]
    d = D_MODEL
    const2 = lambda shape: pl.BlockSpec((None,) + shape, lambda i: (layer,) + (0,) * len(shape),
                                        pipeline_mode=pl.Buffered(1))
    vec = lambda a: a.reshape(a.shape[0], 1, a.shape[-1])
    w_bytes = 2 * (CONV_CH * d + ATTN_OUT_WIDTH * d + POOL_WIDTH * POOL_OUT_GROUP + d * d + 2 * d * ROUTER_LANES)
    act_bytes = tm * (d * 4 + CONV_CH * 2 + ATTN_OUT_WIDTH * 2 + POOL_WIDTH * 2 + 3 * d * 4 + EXT_WIDTH * 4)
    return pl.pallas_call(
        _merge_kernel,
        grid=(m // tm,),
        in_specs=[
            pl.BlockSpec((tm, d), lambda i: (i, 0)),
            pl.BlockSpec((tm, CONV_CH), lambda i: (i, 0)),
            pl.BlockSpec((tm, ATTN_OUT_WIDTH), lambda i: (i, 0)),
            pl.BlockSpec((tm, POOL_WIDTH), lambda i: (i, 0)),
            pl.BlockSpec((tm, d), lambda i: (i, 0)),
            pl.BlockSpec((tm, d), lambda i: (i, 1)),
            pl.BlockSpec((tm, d), lambda i: (i, 2)),
            const2((CONV_CH, d)),
            const2((ATTN_OUT_WIDTH, d)),
            const2((len(POOL_WINDOWS), POOL_GROUP, POOL_OUT_GROUP)),
            const2((1, d)),
            const2((d, d)),
            const2((1, d)),
            const2((d, ROUTER_LANES)),
            const2((d, ROUTER_LANES)),
            const2((1, ROUTER_LANES)),
        ],
        out_specs=pl.BlockSpec((tm, EXT_WIDTH), lambda i: (i, 0)),
        out_shape=jax.ShapeDtypeStruct((m, EXT_WIDTH), F32),
        compiler_params=pltpu.CompilerParams(
            dimension_semantics=("parallel",), vmem_limit_bytes=_vmem_limit(w_bytes + 2 * act_bytes + (16 << 20))),
        name="merge_out_router",
    )(x, a_act, o, cp, zg, zg, zg, wa, wb, wc, vec(c_scale), wo, vec(norm2), wr_hi, wr_lo, vec(b_r))


def _moe_kernel(src_ref, eidx_ref, nvalid_ref, yext_hbm, n2_ref, wg_ref, wu_ref, wd_ref, nf_ref, out_hbm,
                xbuf, hbuf, acc, ybuf, sem_in, sem_out, *, tm, final_norm):
    i = pl.program_id(0)
    c = pl.program_id(1)
    nv = nvalid_ref[i]

    def row_in(r, idx):
        return pltpu.make_async_copy(yext_hbm.at[pl.ds(idx, 1)], xbuf.at[pl.ds(r, 1)], sem_in)

    def row_out(r, idx):
        return pltpu.make_async_copy(ybuf.at[pl.ds(r, 1)], out_hbm.at[pl.ds(idx, 1)], sem_out)

    @pl.when((i == 0) & (c == 0))
    def _():
        xbuf[...] = jnp.zeros(xbuf.shape, F32)

    @pl.when((c == 0) & (nv > 0))
    def _():
        def start(r, carry):
            row_in(r, src_ref[i * tm + r]).start()
            return carry

        def wait(r, carry):
            row_in(r, 0).wait()
            return carry

        lax.fori_loop(0, nv, start, 0)
        lax.fori_loop(0, nv, wait, 0)
        hbuf[...] = _rms(xbuf[:, 0:D_MODEL], n2_ref[...]).astype(BF16)

    @pl.when(nv > 0)
    def _():
        h = hbuf[...]
        hg = jnp.dot(h, wg_ref[...].astype(BF16), preferred_element_type=F32)
        hu = jnp.dot(h, wu_ref[...].astype(BF16), preferred_element_type=F32)
        gv = xbuf[:, D_MODEL:EXT_WIDTH]
        lane = lax.broadcasted_iota(jnp.int32, gv.shape, 1)
        e = eidx_ref[i * EXPERTS_PER_GROUP + c]
        gate = jnp.sum(jnp.where(lane == N_EXPERT_GROUPS + e, gv, 0.0), axis=-1, keepdims=True)
        act = (hg * _sigmoid(hg) * hu * gate).astype(BF16)
        part = jnp.dot(act, wd_ref[...].astype(BF16), preferred_element_type=F32)

        @pl.when(c == 0)
        def _():
            acc[...] = part

        @pl.when(c > 0)
        def _():
            acc[...] = acc[...] + part

    @pl.when((c == EXPERTS_PER_GROUP - 1) & (nv > 0))
    def _():
        y = xbuf[:, 0:D_MODEL] + acc[...]
        if final_norm:
            y = _rms(y, nf_ref[...])
        ybuf[...] = y

        def start(r, carry):
            row_out(r, src_ref[i * tm + r]).start()
            return carry

        def wait(r, carry):
            row_out(r, 0).wait()
            return carry

        lax.fori_loop(0, nv, start, 0)
        lax.fori_loop(0, nv, wait, 0)


def _route(gidx, tm, n_tiles):
    m = gidx.shape[0]
    ng = N_EXPERT_GROUPS
    oh = (gidx[:, None] == jnp.arange(ng, dtype=jnp.int32)[None, :]).astype(jnp.int32)
    csum = jnp.cumsum(oh, axis=0)
    rank = jnp.sum(oh * (csum - 1), axis=1)
    counts = csum[-1]
    tiles_g = (counts + tm - 1) // tm
    tile_off = jnp.concatenate([jnp.zeros((1,), jnp.int32), jnp.cumsum(tiles_g)]).astype(jnp.int32)
    pos = tile_off[gidx] * tm + rank
    src = jnp.zeros((n_tiles * tm,), jnp.int32).at[pos].set(jnp.arange(m, dtype=jnp.int32))
    tid = jnp.arange(n_tiles, dtype=jnp.int32)
    tgrp = jnp.clip(jnp.sum((tid[:, None] >= tile_off[None, 1:]).astype(jnp.int32), axis=1), 0, ng - 1)
    used = tid < tile_off[ng]
    nvalid = jnp.where(used, jnp.clip(counts[tgrp] - (tid - tile_off[tgrp]) * tm, 0, tm), 0).astype(jnp.int32)
    step_c = jnp.arange(EXPERTS_PER_GROUP, dtype=jnp.int32)
    eidx = tgrp[:, None] * EXPERTS_PER_GROUP + step_c[None, :]
    last_used = jnp.maximum(tile_off[ng] - 1, 0)
    last_e = tgrp[last_used] * EXPERTS_PER_GROUP + EXPERTS_PER_GROUP - 1
    eidx = jnp.where(used[:, None], eidx, last_e).reshape(-1).astype(jnp.int32)
    return src, eidx, nvalid


def _moe(yext, norm2, w_gate, w_up, w_down, norm_final, layer, *, tm, final_norm):
    m = yext.shape[0]
    d = D_MODEL
    n_tiles = m // tm + N_EXPERT_GROUPS
    gidx = yext[:, D_MODEL].astype(jnp.int32)
    src, eidx, nvalid = _route(gidx, tm, n_tiles)
    vec = lambda a: a.reshape(a.shape[0], 1, a.shape[-1])
    wmap = lambda i, c, src, eidx, nvalid: (layer, eidx[i * EXPERTS_PER_GROUP + c], 0, 0)
    est = (tm * EXT_WIDTH * 4 + tm * d * 2 + 2 * tm * d * 4 + 2 * 3 * d * EXPERT_FF * 4
           + 3 * d * EXPERT_FF * 2 + 4 * tm * EXPERT_FF * 4 + tm * d * 4)
    grid_spec = pltpu.PrefetchScalarGridSpec(
        num_scalar_prefetch=3,
        grid=(n_tiles, EXPERTS_PER_GROUP),
        in_specs=[
            pl.BlockSpec(memory_space=pl.ANY),
            pl.BlockSpec((None, 1, d), lambda i, c, *_: (layer, 0, 0)),
            pl.BlockSpec((None, None, d, EXPERT_FF), wmap),
            pl.BlockSpec((None, None, d, EXPERT_FF), wmap),
            pl.BlockSpec((None, None, EXPERT_FF, d), wmap),
            pl.BlockSpec((1, d), lambda i, c, *_: (0, 0)),
        ],
        out_specs=pl.BlockSpec(memory_space=pl.ANY),
        scratch_shapes=[
            pltpu.VMEM((tm, EXT_WIDTH), F32),
            pltpu.VMEM((tm, d), BF16),
            pltpu.VMEM((tm, d), F32),
            pltpu.VMEM((tm, d), F32),
            pltpu.SemaphoreType.DMA,
            pltpu.SemaphoreType.DMA,
        ],
    )
    return pl.pallas_call(
        functools.partial(_moe_kernel, tm=tm, final_norm=final_norm),
        grid_spec=grid_spec,
        out_shape=jax.ShapeDtypeStruct((m, d), F32),
        compiler_params=pltpu.CompilerParams(
            dimension_semantics=("arbitrary", "arbitrary"), vmem_limit_bytes=_vmem_limit(est + (8 << 20))),
        name="moe_experts",
    )(src, eidx, nvalid, yext, vec(norm2), w_gate, w_up, w_down, norm_final.reshape(1, d))


def kernel(x_prompt, x_sample, cache_kv_w128, cache_kv_w512, cache_kv_w2048, state_conv, state_pool, norm1, w_in, conv_w, conv_b, ln_g, ln_b, w_a_out, w_b_out, w_c_map, c_scale, w_o, norm2, w_router_group, b_router_group, w_router_expert, b_router_expert, w_gate, w_up, w_down, norm_final):
    nbp, seq, d = x_prompt.shape
    nbs, steps, _ = x_sample.shape
    depth = w_in.shape[0]
    caches = (cache_kv_w128, cache_kv_w512, cache_kv_w2048)
    past_len = PAST_LEN
    for cache, (win, dil) in zip(caches, ATTN_GROUPS):
        assert cache.shape[2] == win and win <= past_len
    assert steps <= min(dil for _, dil in ATTN_GROUPS[1:])

    wa, wb, wc, wo = (w.astype(BF16) for w in (w_a_out, w_b_out, w_c_map, w_o))
    w_r = jnp.concatenate([w_router_group, w_router_expert], axis=-1)
    w_r = jnp.pad(w_r, ((0, 0), (0, 0), (0, ROUTER_LANES - w_r.shape[-1])))
    wr_hi = w_r.astype(BF16)
    wr_lo = (w_r - wr_hi.astype(F32)).astype(BF16)
    b_r = jnp.concatenate([b_router_group, b_router_expert], axis=-1)
    b_r = jnp.pad(b_r, ((0, 0), (0, ROUTER_LANES - b_r.shape[-1])))

    cos_p, sin_p = _rope_tables(jnp.arange(seq, dtype=jnp.int32))
    cos_s, sin_s = _rope_tables(past_len + jnp.arange(steps, dtype=jnp.int32))

    xp = x_prompt.reshape(nbp * seq, d)
    xs = x_sample.transpose(1, 0, 2).reshape(steps * nbs, d)
    sconv_tm = state_conv.transpose(0, 2, 1, 3)
    spool_tm = state_pool.transpose(0, 2, 1, 3)
    c0 = cache_kv_w128
    c1 = cache_kv_w512.reshape(depth, nbs, NK, ATTN_GROUPS[1][1], 2, HEADS_PER_GROUP, HEAD_DIM)
    c2 = cache_kv_w2048.reshape(depth, nbs, NK, ATTN_GROUPS[2][1], 2, HEADS_PER_GROUP, HEAD_DIM)

    hd = (HEADS_PER_GROUP, HEAD_DIM)
    p_conv, p_pool, p_kv = [], [], [[] for _ in ATTN_GROUPS]
    s_conv, s_pool, s_rows = [], [], [[] for _ in ATTN_GROUPS]
    for l in range(depth):
        last = l == depth - 1
        za, zq, zc, zg = _in_proj(xp, norm1, w_in, l, tm=1024)
        a_act, cp, nconv, npool = _branch_ac_prompt(za, zc, conv_w, conv_b, ln_g, ln_b, l, nb=nbp, seq=seq)
        o, k0, k1, k2 = _attn_prompt(zq, cos_p, sin_p, nb=nbp, seq=seq)
        yext = _merge(xp, a_act, o, cp, zg, wa, wb, wc, c_scale, wo, norm2, wr_hi, wr_lo, b_r, l, tm=256)
        xp = _moe(yext, norm2, w_gate, w_up, w_down, norm_final, l, tm=512, final_norm=last)
        p_conv.append(nconv)
        p_pool.append(npool)
        zv = zq[:, 2 * ATTN_WIDTH:].reshape(nbp, seq, N_GROUPS, *hd)
        for g, (kg, (win, dil)) in enumerate(zip((k0, k1, k2), ATTN_GROUPS)):
            keep = min(win, seq)
            kk = kg.reshape(nbp, seq, *hd)[:, seq - keep:]
            vv = zv[:, seq - keep:, g]
            p_kv[g].append(jnp.stack([kk, vv], axis=2))

        za, zq, zc, zg = _in_proj(xs, norm1, w_in, l, tm=steps * nbs)
        a_act, cp, nconv, npool = _branch_ac_sample(za, zc, sconv_tm, spool_tm, conv_w, conv_b, ln_g, ln_b, l,
                                                    nb=nbs, steps=steps, past=past_len)
        qkv5 = zq.reshape(steps, nbs, 3 * N_GROUPS, *hd)
        o, kn = _attn_sample(qkv5, cos_s, sin_s, c0, c1, c2, l, nb=nbs, steps=steps)
        o = o.reshape(steps * nbs, ATTN_OUT_WIDTH).astype(BF16)
        yext = _merge(xs, a_act, o, cp, zg, wa, wb, wc, c_scale, wo, norm2, wr_hi, wr_lo, b_r, l,
                      tm=steps * nbs)
        xs = _moe(yext, norm2, w_gate, w_up, w_down, norm_final, l, tm=steps * nbs, final_norm=last)
        s_conv.append(nconv.transpose(1, 0, 2))
        s_pool.append(npool.transpose(1, 0, 2))
        for g in range(N_GROUPS):
            k_new = kn[:, :, g].transpose(1, 0, 2, 3)
            v_new = qkv5[:, :, 2 * N_GROUPS + g].transpose(1, 0, 2, 3)
            s_rows[g].append(jnp.stack([k_new, v_new], axis=2))

    y_prompt = xp.reshape(nbp, seq, d)
    y_sample = xs.reshape(steps, nbs, d).transpose(1, 0, 2)
    new_kv_sample = [
        jnp.concatenate([cache[:, :, steps:], jnp.stack(s_rows[g], axis=0)], axis=2)
        for g, cache in enumerate(caches)
    ]
    return (y_prompt, y_sample, jnp.stack(p_conv, axis=0), jnp.stack(p_pool, axis=0),
            jnp.stack(p_kv[0], axis=0), jnp.stack(p_kv[1], axis=0), jnp.stack(p_kv[2], axis=0),
            jnp.stack(s_conv, axis=0), jnp.stack(s_pool, axis=0),
            new_kv_sample[0], new_kv_sample[1], new_kv_sample[2])
```

```python
import functools

import jax
import jax.numpy as jnp
from jax import lax
from jax.experimental import pallas as pl
from jax.experimental.pallas import tpu as pltpu

F32 = jnp.float32
BF16 = jnp.bfloat16

D_MODEL = 2048
HEAD_DIM = 128
HEADS_PER_GROUP = 4
ATTN_GROUPS = ((128, 1), (512, 4), (2048, 16))
N_GROUPS = len(ATTN_GROUPS)
ATTN_WIDTH = HEADS_PER_GROUP * N_GROUPS * HEAD_DIM
ATTN_OUT_WIDTH = HEADS_PER_GROUP * HEAD_DIM
ROPE_THETA = 10000.0
CONV_CH = D_MODEL // 2
DW_CONV_LEN = 31
CONV_HALO = DW_CONV_LEN - 1
POOL_WINDOWS = (2, 4, 8, 16)
POOL_GROUP = D_MODEL // 8
POOL_WIDTH = len(POOL_WINDOWS) * POOL_GROUP
POOL_OUT_GROUP = D_MODEL // len(POOL_WINDOWS)
POOL_HALO = max(POOL_WINDOWS) - 1
N_BRANCHES = 3
N_EXPERT_GROUPS = 4
EXPERTS_PER_GROUP = 4
N_EXPERTS = N_EXPERT_GROUPS * EXPERTS_PER_GROUP
EXPERT_FF = D_MODEL // 4
NORM_EPS = 1e-6
PAST_LEN = 8192
NK = 128

V7X_VMEM_BYTES = 64 * 1024 * 1024
LANES = 128
ROUTER_LANES = LANES
EXT_WIDTH = D_MODEL + ROUTER_LANES
IN_TILE_N = 512
NEG_INF = float("-inf")
ROW_UNROLL = 8
ATTN_BLOCK_UNROLL = 5


def _vmem_limit(nbytes):
    return int(min(V7X_VMEM_BYTES - (6 << 20), max(nbytes, 16 << 20)))


def _rms(x, g):
    return x * lax.rsqrt(jnp.mean(x * x, axis=-1, keepdims=True) + NORM_EPS) * g


def _sigmoid(x):
    return 1.0 / (1.0 + jnp.exp(-x))


def _rope_tables(pos):
    half = HEAD_DIM // 2
    inv = ROPE_THETA ** (-jnp.arange(half, dtype=F32) / half)
    ang = pos.astype(F32)[:, None] * inv[None, :]
    c, s = jnp.cos(ang), jnp.sin(ang)
    return jnp.concatenate([c, c], axis=-1), jnp.concatenate([-s, s], axis=-1)


def _rope(x, cos2, sin2):
    return x * cos2 + pltpu.roll(x, HEAD_DIM // 2, x.ndim - 1) * sin2


Z_QKV = 2 * CONV_CH
Z_POOL = Z_QKV + 3 * ATTN_WIDTH
Z_GATE = Z_POOL + POOL_WIDTH
IN_WIDTH = Z_GATE + N_BRANCHES * D_MODEL


def _prenorm_kernel(x_ref, g_ref, h_ref):
    h_ref[...] = _rms(x_ref[...], g_ref[...]).astype(BF16)


def _prenorm(x, norm_g, layer, *, tm):
    m, d = x.shape
    return pl.pallas_call(
        _prenorm_kernel,
        grid=(m // tm,),
        in_specs=[
            pl.BlockSpec((tm, d), lambda i: (i, 0)),
            pl.BlockSpec((None, 1, d), lambda i: (layer, 0, 0)),
        ],
        out_specs=pl.BlockSpec((tm, d), lambda i: (i, 0)),
        out_shape=jax.ShapeDtypeStruct((m, d), BF16),
        compiler_params=pltpu.CompilerParams(
            dimension_semantics=("parallel",), vmem_limit_bytes=_vmem_limit(4 * tm * d * 4)),
        name="prenorm",
    )(x, norm_g.reshape(norm_g.shape[0], 1, d))


def _in_proj_kernel(h_ref, w_ref, z_ref, wbf, *, tm):
    i = pl.program_id(1)

    @pl.when(i == 0)
    def _():
        wbf[...] = w_ref[...].astype(BF16)

    rows = pl.ds(pl.multiple_of(i * tm, tm), tm)
    z_ref[...] = jnp.dot(h_ref[rows, :], wbf[...], preferred_element_type=F32)


def _in_proj(h, w_in, layer, *, tm):
    m, d = h.shape
    n = w_in.shape[-1]
    tn = IN_TILE_N
    est = m * d * 2 + 2 * d * tn * 4 + d * tn * 2 + 2 * tm * tn * 4
    return pl.pallas_call(
        functools.partial(_in_proj_kernel, tm=tm),
        grid=(n // tn, m // tm),
        in_specs=[
            pl.BlockSpec((m, d), lambda j, i: (0, 0), pipeline_mode=pl.Buffered(1)),
            pl.BlockSpec((None, d, tn), lambda j, i: (layer, 0, j)),
        ],
        out_specs=pl.BlockSpec((tm, tn), lambda j, i: (i, j)),
        out_shape=jax.ShapeDtypeStruct((m, n), F32),
        scratch_shapes=[pltpu.VMEM((d, tn), BF16)],
        compiler_params=pltpu.CompilerParams(
            dimension_semantics=("arbitrary", "arbitrary"), vmem_limit_bytes=_vmem_limit(est + (8 << 20))),
        name="in_proj",
    )(h, w_in)


CONV_PAD = 32
POOL_PAD = 16


def _conv_ln_silu(ubuf, abuf, w_ref, b_ref, lg_ref, lb_ref, t_rows, row_chunk, col_chunk):
    off = CONV_PAD - CONV_HALO
    for r0 in range(0, t_rows, row_chunk):
        for c0 in range(0, CONV_CH, col_chunk):
            acc = jnp.zeros((row_chunk, col_chunk), F32) + b_ref[:, c0:c0 + col_chunk]
            for k in range(DW_CONV_LEN):
                acc = acc + w_ref[k:k + 1, c0:c0 + col_chunk] * ubuf[r0 + off + k:r0 + off + k + row_chunk,
                                                                    c0:c0 + col_chunk]
            abuf[r0:r0 + row_chunk, c0:c0 + col_chunk] = acc
    a = abuf[...]
    mu = jnp.mean(a, axis=-1, keepdims=True)
    xc = a - mu
    var = jnp.mean(xc * xc, axis=-1, keepdims=True)
    y = xc * lax.rsqrt(var + NORM_EPS) * lg_ref[...] + lb_ref[...]
    return y * _sigmoid(y)


def _branch_ac_prompt_kernel(za1_ref, za2_ref, zc0_ref, zc1_ref, w_ref, b_ref, lg_ref, lb_ref,
                             a_ref, cp_ref, nconv_ref, npool_ref, ubuf, abuf, cbuf, *, t_rows):
    t = pl.program_id(1)
    nt = pl.num_programs(1)

    @pl.when(t == 0)
    def _():
        ubuf[0:CONV_PAD, :] = jnp.zeros((CONV_PAD, CONV_CH), F32)
        cbuf[0:POOL_PAD, :] = jnp.zeros((POOL_PAD, POOL_WIDTH), F32)

    ubuf[CONV_PAD:CONV_PAD + t_rows, :] = za1_ref[...] * _sigmoid(za2_ref[...])
    a_ref[...] = _conv_ln_silu(ubuf, abuf, w_ref, b_ref, lg_ref, lb_ref, t_rows, 64, 512).astype(BF16)

    half = POOL_WIDTH // 2
    cbuf[POOL_PAD:POOL_PAD + t_rows, 0:half] = zc0_ref[...]
    cbuf[POOL_PAD:POOL_PAD + t_rows, half:POOL_WIDTH] = zc1_ref[...]
    rc = 128
    for r0 in range(0, t_rows, rc):
        pos = t * t_rows + r0 + lax.broadcasted_iota(jnp.int32, (rc, 1), 0)
        for g, w in enumerate(POOL_WINDOWS):
            cs = slice(g * POOL_GROUP, (g + 1) * POOL_GROUP)
            s = cbuf[POOL_PAD + r0:POOL_PAD + r0 + rc, cs]
            cur = s
            for i in range(1, w):
                s = s + cbuf[POOL_PAD + r0 - i:POOL_PAD + r0 - i + rc, cs]
            cnt = jnp.minimum(w, pos + 1).astype(F32)
            cp_ref[r0:r0 + rc, cs] = (s / cnt - cur).astype(BF16)

    @pl.when(t == nt - 1)
    def _():
        tail = slice(t_rows - CONV_HALO, t_rows)
        nconv_ref[...] = za1_ref[tail, :] * _sigmoid(za2_ref[tail, :])
        npool_ref[...] = cbuf[POOL_PAD + t_rows - POOL_HALO:POOL_PAD + t_rows, :]

    ubuf[0:CONV_PAD, :] = ubuf[t_rows:t_rows + CONV_PAD, :]
    cbuf[0:POOL_PAD, :] = cbuf[t_rows:t_rows + POOL_PAD, :]


def _branch_ac_prompt(z, conv_w, conv_b, ln_g, ln_b, layer, *, nb, seq, t_rows=256):
    m = nb * seq
    nt = seq // t_rows
    half = POOL_WIDTH // 2
    assert Z_POOL % half == 0
    vec = lambda a: a.reshape(a.shape[0], 1, a.shape[-1])
    vspec = pl.BlockSpec((None, 1, CONV_CH), lambda b, t: (layer, 0, 0))
    return pl.pallas_call(
        functools.partial(_branch_ac_prompt_kernel, t_rows=t_rows),
        grid=(nb, nt),
        in_specs=[
            pl.BlockSpec((t_rows, CONV_CH), lambda b, t: (b * nt + t, 0)),
            pl.BlockSpec((t_rows, CONV_CH), lambda b, t: (b * nt + t, 1)),
            pl.BlockSpec((t_rows, half), lambda b, t: (b * nt + t, Z_POOL // half)),
            pl.BlockSpec((t_rows, half), lambda b, t: (b * nt + t, Z_POOL // half + 1)),
            pl.BlockSpec((None, DW_CONV_LEN, CONV_CH), lambda b, t: (layer, 0, 0)),
            vspec, vspec, vspec,
        ],
        out_specs=[
            pl.BlockSpec((t_rows, CONV_CH), lambda b, t: (b * nt + t, 0)),
            pl.BlockSpec((t_rows, POOL_WIDTH), lambda b, t: (b * nt + t, 0)),
            pl.BlockSpec((None, CONV_HALO, CONV_CH), lambda b, t: (b, 0, 0)),
            pl.BlockSpec((None, POOL_HALO, POOL_WIDTH), lambda b, t: (b, 0, 0)),
        ],
        out_shape=[
            jax.ShapeDtypeStruct((m, CONV_CH), BF16),
            jax.ShapeDtypeStruct((m, POOL_WIDTH), BF16),
            jax.ShapeDtypeStruct((nb, CONV_HALO, CONV_CH), F32),
            jax.ShapeDtypeStruct((nb, POOL_HALO, POOL_WIDTH), F32),
        ],
        scratch_shapes=[
            pltpu.VMEM((CONV_PAD + t_rows, CONV_CH), F32),
            pltpu.VMEM((t_rows, CONV_CH), F32),
            pltpu.VMEM((POOL_PAD + t_rows, POOL_WIDTH), F32),
        ],
        compiler_params=pltpu.CompilerParams(
            dimension_semantics=("parallel", "arbitrary"), vmem_limit_bytes=_vmem_limit(32 << 20)),
        name="branch_ac_prompt",
    )(z, z, z, z, conv_w, vec(conv_b), vec(ln_g), vec(ln_b))


def _branch_ac_sample_kernel(za1_ref, za2_ref, zc0_ref, zc1_ref, sconv_ref, spool_ref, w_ref, b_ref, lg_ref, lb_ref,
                             a_ref, cp_ref, nconv_ref, npool_ref, uext, cext, *, nb, steps, past):
    for i in range(CONV_HALO):
        uext[i] = sconv_ref[i]
    for j in range(steps):
        rows = slice(j * nb, (j + 1) * nb)
        uext[CONV_HALO + j] = za1_ref[rows, :] * _sigmoid(za2_ref[rows, :])
    for j in range(steps):
        acc = jnp.zeros((nb, CONV_CH), F32) + b_ref[...]
        for k in range(DW_CONV_LEN):
            acc = acc + w_ref[k:k + 1, :] * uext[j + k]
        mu = jnp.mean(acc, axis=-1, keepdims=True)
        xc = acc - mu
        var = jnp.mean(xc * xc, axis=-1, keepdims=True)
        y = xc * lax.rsqrt(var + NORM_EPS) * lg_ref[...] + lb_ref[...]
        a_ref[j * nb:(j + 1) * nb, :] = (y * _sigmoid(y)).astype(BF16)
    for i in range(CONV_HALO):
        nconv_ref[i] = uext[steps + i]

    for i in range(POOL_HALO):
        cext[i] = spool_ref[i]
    half = POOL_WIDTH // 2
    for j in range(steps):
        cext[POOL_HALO + j, :, 0:half] = zc0_ref[j * nb:(j + 1) * nb, :]
        cext[POOL_HALO + j, :, half:POOL_WIDTH] = zc1_ref[j * nb:(j + 1) * nb, :]
    for j in range(steps):
        for g, w in enumerate(POOL_WINDOWS):
            cs = slice(g * POOL_GROUP, (g + 1) * POOL_GROUP)
            cur = cext[POOL_HALO + j, :, cs]
            s = cur
            for i in range(1, w):
                s = s + cext[POOL_HALO + j - i, :, cs]
            cnt = float(min(w, past + j + 1))
            cp_ref[j * nb:(j + 1) * nb, cs] = (s / cnt - cur).astype(BF16)
    for i in range(POOL_HALO):
        npool_ref[i] = cext[steps + i]


def _branch_ac_sample(z, sconv_tm, spool_tm, conv_w, conv_b, ln_g, ln_b, layer, *, nb, steps, past):
    m = nb * steps
    half = POOL_WIDTH // 2
    vec = lambda a: a.reshape(a.shape[0], 1, a.shape[-1])
    vspec = pl.BlockSpec((None, 1, CONV_CH), lambda i: (layer, 0, 0))
    return pl.pallas_call(
        functools.partial(_branch_ac_sample_kernel, nb=nb, steps=steps, past=past),
        grid=(1,),
        in_specs=[
            pl.BlockSpec((m, CONV_CH), lambda i: (0, 0)),
            pl.BlockSpec((m, CONV_CH), lambda i: (0, 1)),
            pl.BlockSpec((m, half), lambda i: (0, Z_POOL // half)),
            pl.BlockSpec((m, half), lambda i: (0, Z_POOL // half + 1)),
            pl.BlockSpec((None, CONV_HALO, nb, CONV_CH), lambda i: (layer, 0, 0, 0)),
            pl.BlockSpec((None, POOL_HALO, nb, POOL_WIDTH), lambda i: (layer, 0, 0, 0)),
            pl.BlockSpec((None, DW_CONV_LEN, CONV_CH), lambda i: (layer, 0, 0)),
            vspec, vspec, vspec,
        ],
        out_specs=[
            pl.BlockSpec((m, CONV_CH), lambda i: (0, 0)),
            pl.BlockSpec((m, POOL_WIDTH), lambda i: (0, 0)),
            pl.BlockSpec((CONV_HALO, nb, CONV_CH), lambda i: (0, 0, 0)),
            pl.BlockSpec((POOL_HALO, nb, POOL_WIDTH), lambda i: (0, 0, 0)),
        ],
        out_shape=[
            jax.ShapeDtypeStruct((m, CONV_CH), BF16),
            jax.ShapeDtypeStruct((m, POOL_WIDTH), BF16),
            jax.ShapeDtypeStruct((CONV_HALO, nb, CONV_CH), F32),
            jax.ShapeDtypeStruct((POOL_HALO, nb, POOL_WIDTH), F32),
        ],
        scratch_shapes=[
            pltpu.VMEM((CONV_HALO + steps, nb, CONV_CH), F32),
            pltpu.VMEM((POOL_HALO + steps, nb, POOL_WIDTH), F32),
        ],
        compiler_params=pltpu.CompilerParams(vmem_limit_bytes=_vmem_limit(48 << 20)),
        name="branch_ac_sample",
    )(z, z, z, z, sconv_tm, spool_tm, conv_w, vec(conv_b), vec(ln_g), vec(ln_b))


def _attn_prompt_kernel(*refs, seq):
    qkv_refs = refs[0:9]
    cos_ref, sin_ref = refs[9:11]
    o_ref = refs[11]
    ko_refs = refs[12:15]
    qs, og, lse = refs[15:18]
    scale = HEAD_DIM ** -0.5

    cos2 = cos_ref[...]
    sin2 = sin_ref[...]
    for g in range(N_GROUPS):
        qs[g] = _rope(qkv_refs[3 * g][...], cos2, sin2)
        ko_refs[g][...] = _rope(qkv_refs[3 * g + 1][...], cos2, sin2)

    def rows(start, count, dil):
        if dil == 1:
            return pl.ds(start if isinstance(start, int) else pl.multiple_of(start, NK), count)
        return pl.ds(start, count, stride=dil)

    def block(g, dil, q_start, k_start, nkeys):
        k_ref, v_ref = ko_refs[g], qkv_refs[3 * g + 2]
        q = qs[g, rows(q_start, NK, dil), :].astype(BF16)
        k = k_ref[rows(k_start, nkeys, dil), :].astype(BF16)
        v = v_ref[rows(k_start, nkeys, dil), :].astype(BF16)
        s = lax.dot_general(q, k, (((1,), (1,)), ((), ())), preferred_element_type=F32) * scale
        qi = lax.broadcasted_iota(jnp.int32, (NK, nkeys), 0)
        kk = lax.broadcasted_iota(jnp.int32, (NK, nkeys), 1)
        rel = qi + (nkeys - NK) - kk
        s = jnp.where((rel >= 0) & (rel <= NK), s, NEG_INF)
        mx = jnp.max(s, axis=-1, keepdims=True)
        e = jnp.exp(s - mx)
        l = jnp.sum(e, axis=-1, keepdims=True)
        p = (e * (1.0 / l)).astype(BF16)
        og[g, rows(q_start, NK, dil), :] = jnp.dot(p, v, preferred_element_type=F32)
        lse[g, rows(q_start, NK, dil), :] = jnp.broadcast_to(mx + jnp.log(l), (NK, HEAD_DIM))

    for g, (win, dil) in enumerate(ATTN_GROUPS):
        span = NK * dil
        nblk = seq // span

        def per_residue(r, carry, g=g, dil=dil, span=span, nblk=nblk):
            block(g, dil, r, r, NK)

            def per_block(n, c):
                block(g, dil, n * span + r, (n - 1) * span + r, 2 * NK)
                return c

            if nblk > 1:
                lax.fori_loop(1, nblk, per_block, 0, unroll=ATTN_BLOCK_UNROLL if nblk > 4 else True)
            return carry

        if dil == 1:
            per_residue(0, 0)
        else:
            lax.fori_loop(0, dil, per_residue, 0, unroll=(ATTN_BLOCK_UNROLL - 1) if nblk == 1 else 1)

    m = jnp.maximum(jnp.maximum(lse[0], lse[1]), lse[2])
    num = jnp.zeros((seq, HEAD_DIM), F32)
    den = jnp.zeros((seq, HEAD_DIM), F32)
    for g in range(N_GROUPS):
        w = jnp.exp(lse[g] - m)
        num = num + w * og[g]
        den = den + w
    o_ref[...] = (num / den).astype(BF16)


def _attn_prompt(z, cos2, sin2, *, nb, seq):
    m = nb * seq
    for win, dil in ATTN_GROUPS:
        assert win // dil == NK and seq % (NK * dil) == 0
    nh = HEADS_PER_GROUP * N_GROUPS

    def col(kind, g):
        return lambda b, j: (b, Z_QKV // HEAD_DIM + kind * nh + g * HEADS_PER_GROUP + j)

    in_specs = []
    for g in range(N_GROUPS):
        for kind in range(3):
            in_specs.append(pl.BlockSpec((seq, HEAD_DIM), col(kind, g)))
    in_specs += [pl.BlockSpec((seq, HEAD_DIM), lambda b, j: (0, 0))] * 2
    blk = seq * HEAD_DIM * 4
    est = 2 * 9 * blk + 4 * blk + 2 * blk // 2 + 2 * 3 * blk + 9 * blk
    return pl.pallas_call(
        functools.partial(_attn_prompt_kernel, seq=seq),
        grid=(nb, HEADS_PER_GROUP),
        in_specs=in_specs,
        out_specs=[pl.BlockSpec((seq, HEAD_DIM), lambda b, j: (b, j))] * 4,
        out_shape=[jax.ShapeDtypeStruct((m, ATTN_OUT_WIDTH), BF16)]
        + [jax.ShapeDtypeStruct((m, ATTN_OUT_WIDTH), F32)] * 3,
        scratch_shapes=[pltpu.VMEM((N_GROUPS, seq, HEAD_DIM), F32)] * 3,
        compiler_params=pltpu.CompilerParams(
            dimension_semantics=("parallel", "arbitrary"), vmem_limit_bytes=_vmem_limit(est + (8 << 20))),
        name="attn_prompt",
    )(*([z] * 9), cos2, sin2)


def _attn_sample_kernel(qkv_ref, cos_ref, sin_ref, c0_ref, c1_ref, c2_ref, s0_hbm, s1_hbm, s2_hbm,
                        o_ref, n0_ref, n1_ref, n2_ref, *, steps):
    del s0_hbm, s1_hbm, s2_hbm
    new_refs = (n0_ref, n1_ref, n2_ref)
    scale = HEAD_DIM ** -0.5
    ng = N_GROUPS
    q = [[None] * ng for _ in range(steps)]
    kn = [[None] * ng for _ in range(steps)]
    for j in range(steps):
        cos2 = cos_ref[j:j + 1, :]
        sin2 = sin_ref[j:j + 1, :]
        for g in range(ng):
            q[j][g] = _rope(qkv_ref[j, g], cos2, sin2)
            k_new = _rope(qkv_ref[j, ng + g], cos2, sin2)
            kn[j][g] = k_new
            new_refs[g][j, 0] = k_new
            new_refs[g][j, 1] = qkv_ref[j, 2 * ng + g]

    for j in range(steps):
        parts = []
        for g, (win, dil) in enumerate(ATTN_GROUPS):
            if g == 0:
                kc, vc = c0_ref[:, 0], c0_ref[:, 1]
            elif g == 1:
                kc, vc = c1_ref[:, j, 0], c1_ref[:, j, 1]
            else:
                kc, vc = c2_ref[:, j, 0], c2_ref[:, j, 1]
            qv = q[j][g]
            s = jnp.sum(kc * qv[None], axis=-1, keepdims=True) * scale
            if dil == 1:
                ri = lax.broadcasted_iota(jnp.int32, s.shape, 0)
                s = jnp.where(ri >= j, s, NEG_INF)
                new_js = list(range(j + 1))
            else:
                new_js = [j]
            s_new = [jnp.sum(kn[jn][g] * qv, axis=-1, keepdims=True) * scale for jn in new_js]
            mx = jnp.max(s, axis=0)
            for sn in s_new:
                mx = jnp.maximum(mx, sn)
            e = jnp.exp(s - mx[None])
            e_new = [jnp.exp(sn - mx) for sn in s_new]
            l = jnp.sum(e, axis=0)
            for en in e_new:
                l = l + en
            inv_l = 1.0 / l
            acc = jnp.sum((e * inv_l[None]) * vc, axis=0)
            for jn, en in zip(new_js, e_new):
                acc = acc + (en * inv_l) * qkv_ref[jn, 2 * ng + g]
            parts.append((acc, mx + jnp.log(l)))
        m = jnp.maximum(jnp.maximum(parts[0][1], parts[1][1]), parts[2][1])
        num = jnp.zeros((HEADS_PER_GROUP, HEAD_DIM), F32)
        den = jnp.zeros((HEADS_PER_GROUP, 1), F32)
        for o_g, lse_g in parts:
            w = jnp.exp(lse_g - m)
            num = num + w * o_g
            den = den + w
        o_ref[j] = num / den


def _attn_sample(qkv5, cos2, sin2, c0, c1, c2, shifted, layer, *, nb, steps):
    hd = (HEADS_PER_GROUP, HEAD_DIM)

    def newest(arr):
        assert arr.shape[2] % steps == 0
        last = arr.shape[2] // steps - 1
        return pl.BlockSpec((None, None, steps, 2) + hd, lambda b: (layer, b, last, 0, 0, 0))

    return pl.pallas_call(
        functools.partial(_attn_sample_kernel, steps=steps),
        grid=(nb,),
        in_specs=[
            pl.BlockSpec((steps, None, 3 * N_GROUPS) + hd, lambda b: (0, b, 0, 0, 0)),
            pl.BlockSpec((steps, HEAD_DIM), lambda b: (0, 0)),
            pl.BlockSpec((steps, HEAD_DIM), lambda b: (0, 0)),
            pl.BlockSpec((None, None, NK, 2) + hd, lambda b: (layer, b, 0, 0, 0, 0)),
            pl.BlockSpec((None, None, NK, steps, 2) + hd, lambda b: (layer, b, 0, 0, 0, 0, 0)),
            pl.BlockSpec((None, None, NK, steps, 2) + hd, lambda b: (layer, b, 0, 0, 0, 0, 0)),
        ] + [pl.BlockSpec(memory_space=pl.ANY)] * N_GROUPS,
        out_specs=[pl.BlockSpec((steps, None) + hd, lambda b: (0, b, 0, 0))] + [newest(a) for a in shifted],
        out_shape=[jax.ShapeDtypeStruct((steps, nb) + hd, F32)]
        + [jax.ShapeDtypeStruct(a.shape, a.dtype) for a in shifted],
        input_output_aliases={6 + g: 1 + g for g in range(N_GROUPS)},
        compiler_params=pltpu.CompilerParams(
            dimension_semantics=("parallel",), vmem_limit_bytes=_vmem_limit(40 << 20)),
        name="attn_sample",
    )(qkv5, cos2, sin2, c0, c1, c2, *shifted)


def _cache_shift_kernel(*refs, steps, nchunk):
    n = N_GROUPS
    srcs, dsts = refs[0:n], refs[n:2 * n]
    zbuf, sem, zsem = refs[2 * n:]
    zbuf[...] = jnp.zeros(zbuf.shape, F32)
    copies = []
    for src, dst in zip(srcs, dsts):
        depth, nb, length = src.shape[:3]
        bs = nb // nchunk
        for l in range(depth):
            for k in range(nchunk):
                copies.append(pltpu.make_async_copy(
                    src.at[l, pl.ds(k * bs, bs), pl.ds(steps, length - steps)],
                    dst.at[l, pl.ds(k * bs, bs), pl.ds(0, length - steps)], sem))
            copies.append(pltpu.make_async_copy(zbuf, dst.at[l, :, pl.ds(length - steps, steps)], zsem))
    for cp in copies:
        cp.start()
    for cp in copies:
        cp.wait()


def _cache_shift(caches, *, steps, nchunk=4):
    nb = caches[0].shape[1]
    return pl.pallas_call(
        functools.partial(_cache_shift_kernel, steps=steps, nchunk=nchunk),
        in_specs=[pl.BlockSpec(memory_space=pl.ANY)] * N_GROUPS,
        out_specs=[pl.BlockSpec(memory_space=pl.ANY)] * N_GROUPS,
        out_shape=[jax.ShapeDtypeStruct(c.shape, c.dtype) for c in caches],
        scratch_shapes=[
            pltpu.VMEM((nb, steps, 2, HEADS_PER_GROUP, HEAD_DIM), F32),
            pltpu.SemaphoreType.DMA,
            pltpu.SemaphoreType.DMA,
        ],
        name="cache_shift",
    )(*caches)


def _merge_kernel(*refs):
    x_ref, a_ref, o_ref, cp_ref = refs[0:4]
    ng = len(POOL_WINDOWS)
    gate_refs = refs[4:4 + N_BRANCHES * ng]
    wa_ref, wb_ref, wc_ref, cs_ref, wo_ref, n2_ref, wr_ref, br_ref, y_ref = refs[4 + N_BRANCHES * ng:]
    a = jnp.dot(a_ref[...], wa_ref[...], preferred_element_type=F32)
    bb = jnp.dot(o_ref[...], wb_ref[...], preferred_element_type=F32)
    cs = cs_ref[...]
    for g in range(ng):
        oc = slice(g * POOL_OUT_GROUP, (g + 1) * POOL_OUT_GROUP)
        cb = jnp.dot(cp_ref[:, g * POOL_GROUP:(g + 1) * POOL_GROUP], wc_ref[g], preferred_element_type=F32)
        y_ref[:, oc] = (_sigmoid(gate_refs[g][...]) * a[:, oc] + _sigmoid(gate_refs[ng + g][...]) * bb[:, oc]
                        + _sigmoid(gate_refs[2 * ng + g][...]) * (cb * cs[:, oc]))
    mrg = y_ref[:, 0:D_MODEL].astype(BF16)
    y = x_ref[...] + jnp.dot(mrg, wo_ref[...], preferred_element_type=F32)
    y_ref[:, 0:D_MODEL] = y

    logits = jnp.dot(_rms(y, n2_ref[...]), wr_ref[...], preferred_element_type=F32,
                     precision=lax.Precision.HIGHEST) + br_ref[...]
    lane = lax.broadcasted_iota(jnp.int32, logits.shape, 1).astype(F32)
    far = float(ROUTER_LANES)
    is_g = lane < N_EXPERT_GROUPS
    gmax = jnp.max(jnp.where(is_g, logits, NEG_INF), axis=-1, keepdims=True)
    g_idx = jnp.min(jnp.where(is_g & (logits == gmax), lane, far), axis=-1, keepdims=True)
    p_group = 1.0 / jnp.sum(jnp.where(is_g, jnp.exp(logits - gmax), 0.0), axis=-1, keepdims=True)
    e_lo = N_EXPERT_GROUPS + g_idx * EXPERTS_PER_GROUP
    sel = (lane >= e_lo) & (lane < e_lo + EXPERTS_PER_GROUP)
    t1 = jnp.max(jnp.where(sel, logits, NEG_INF), axis=-1, keepdims=True)
    i1 = jnp.min(jnp.where(sel & (logits == t1), lane, far), axis=-1, keepdims=True)
    sel2 = sel & (lane != i1)
    t2 = jnp.max(jnp.where(sel2, logits, NEG_INF), axis=-1, keepdims=True)
    i2 = jnp.min(jnp.where(sel2 & (logits == t2), lane, far), axis=-1, keepdims=True)
    r = jnp.exp(t2 - t1)
    w1 = p_group / (1.0 + r)
    w2 = p_group * r / (1.0 + r)
    gate = jnp.where(lane == i1, w1, 0.0) + jnp.where(lane == i2, w2, 0.0)
    y_ref[:, D_MODEL:EXT_WIDTH] = jnp.where(lane == 0.0, g_idx, gate)


def _merge(x, a_act, o, cp, z, wa, wb, wc, c_scale, wo, norm2, wr, b_r, layer, *, tm, row0=0):
    m = x.shape[0]
    d = D_MODEL
    ng = len(POOL_WINDOWS)
    assert Z_GATE % POOL_OUT_GROUP == 0 and d == ng * POOL_OUT_GROUP and row0 % tm == 0

    def gate_spec(s, g):
        col = Z_GATE // POOL_OUT_GROUP + s * ng + g
        return pl.BlockSpec((tm, POOL_OUT_GROUP), lambda i: (row0 // tm + i, col))

    const2 = lambda shape: pl.BlockSpec((None,) + shape, lambda i: (layer,) + (0,) * len(shape),
                                        pipeline_mode=pl.Buffered(1))
    vec = lambda a: a.reshape(a.shape[0], 1, a.shape[-1])
    w_bytes = 2 * (CONV_CH * d + ATTN_OUT_WIDTH * d + POOL_WIDTH * POOL_OUT_GROUP + d * d + d * ROUTER_LANES)
    act_bytes = tm * (d * 4 + CONV_CH * 2 + ATTN_OUT_WIDTH * 2 + POOL_WIDTH * 2 + 3 * d * 4 + EXT_WIDTH * 4)
    return pl.pallas_call(
        _merge_kernel,
        grid=(m // tm,),
        in_specs=[
            pl.BlockSpec((tm, d), lambda i: (i, 0)),
            pl.BlockSpec((tm, CONV_CH), lambda i: (i, 0)),
            pl.BlockSpec((tm, ATTN_OUT_WIDTH), lambda i: (i, 0)),
            pl.BlockSpec((tm, POOL_WIDTH), lambda i: (i, 0)),
        ] + [gate_spec(s, g) for s in range(N_BRANCHES) for g in range(ng)] + [
            const2((CONV_CH, d)),
            const2((ATTN_OUT_WIDTH, d)),
            const2((len(POOL_WINDOWS), POOL_GROUP, POOL_OUT_GROUP)),
            const2((1, d)),
            const2((d, d)),
            const2((1, d)),
            const2((d, ROUTER_LANES)),
            const2((1, ROUTER_LANES)),
        ],
        out_specs=pl.BlockSpec((tm, EXT_WIDTH), lambda i: (i, 0)),
        out_shape=jax.ShapeDtypeStruct((m, EXT_WIDTH), F32),
        compiler_params=pltpu.CompilerParams(
            dimension_semantics=("parallel",), vmem_limit_bytes=_vmem_limit(w_bytes + 2 * act_bytes + (16 << 20))),
        name="merge_out_router",
    )(x, a_act, o, cp, *([z] * (N_BRANCHES * ng)), wa, wb, wc, vec(c_scale), wo, vec(norm2), wr, vec(b_r))


def _moe_kernel(src_ref, eidx_ref, nvalid_ref, yext_hbm, n2_ref, wg_ref, wu_ref, wd_ref, nf_ref, out_hbm,
                xbuf, hbuf, acc, ybuf, sem_in, sem_out, *, tm, final_norm):
    i = pl.program_id(0)
    c = pl.program_id(1)
    nv = nvalid_ref[i]

    def row_in(r, idx):
        return pltpu.make_async_copy(yext_hbm.at[pl.ds(idx, 1)], xbuf.at[pl.ds(r, 1)], sem_in)

    def row_out(r, idx):
        return pltpu.make_async_copy(ybuf.at[pl.ds(r, 1)], out_hbm.at[pl.ds(idx, 1)], sem_out)

    def for_rows(count, fn):
        full = count // ROW_UNROLL

        def chunk(q, carry):
            for u in range(ROW_UNROLL):
                fn(q * ROW_UNROLL + u)
            return carry

        def single(r, carry):
            fn(r)
            return carry

        lax.fori_loop(0, full, chunk, 0)
        lax.fori_loop(full * ROW_UNROLL, count, single, 0)

    @pl.when((i == 0) & (c == 0))
    def _():
        xbuf[...] = jnp.zeros(xbuf.shape, F32)

    @pl.when((c == 0) & (nv > 0))
    def _():
        for_rows(nv, lambda r: row_in(r, src_ref[i * tm + r]).start())
        for_rows(nv, lambda r: row_in(r, 0).wait())
        hbuf[...] = _rms(xbuf[:, 0:D_MODEL], n2_ref[...]).astype(BF16)

    @pl.when(nv > 0)
    def _():
        h = hbuf[...]
        hg = jnp.dot(h, wg_ref[...].astype(BF16), preferred_element_type=F32)
        hu = jnp.dot(h, wu_ref[...].astype(BF16), preferred_element_type=F32)
        gv = xbuf[:, D_MODEL:EXT_WIDTH]
        lane = lax.broadcasted_iota(jnp.int32, gv.shape, 1)
        e = eidx_ref[i * EXPERTS_PER_GROUP + c]
        gate = jnp.sum(jnp.where(lane == N_EXPERT_GROUPS + e, gv, 0.0), axis=-1, keepdims=True)
        act = (hg * _sigmoid(hg) * hu * gate).astype(BF16)
        part = jnp.dot(act, wd_ref[...].astype(BF16), preferred_element_type=F32)

        @pl.when(c == 0)
        def _():
            acc[...] = part

        @pl.when(c > 0)
        def _():
            acc[...] = acc[...] + part

    @pl.when((c == EXPERTS_PER_GROUP - 1) & (nv > 0))
    def _():
        y = xbuf[:, 0:D_MODEL] + acc[...]
        if final_norm:
            y = _rms(y, nf_ref[...])
        ybuf[...] = y

        for_rows(nv, lambda r: row_out(r, src_ref[i * tm + r]).start())
        for_rows(nv, lambda r: row_out(r, 0).wait())


def _route(gidx, tm, n_tiles):
    m = gidx.shape[0]
    ng = N_EXPERT_GROUPS
    oh = (gidx[:, None] == jnp.arange(ng, dtype=jnp.int32)[None, :]).astype(jnp.int32)
    csum = jnp.cumsum(oh, axis=0)
    rank = jnp.sum(oh * (csum - 1), axis=1)
    counts = csum[-1]
    tiles_g = (counts + tm - 1) // tm
    tile_off = jnp.concatenate([jnp.zeros((1,), jnp.int32), jnp.cumsum(tiles_g)]).astype(jnp.int32)
    pos = tile_off[gidx] * tm + rank
    src = jnp.zeros((n_tiles * tm,), jnp.int32).at[pos].set(jnp.arange(m, dtype=jnp.int32))
    tid = jnp.arange(n_tiles, dtype=jnp.int32)
    tgrp = jnp.clip(jnp.sum((tid[:, None] >= tile_off[None, 1:]).astype(jnp.int32), axis=1), 0, ng - 1)
    used = tid < tile_off[ng]
    nvalid = jnp.where(used, jnp.clip(counts[tgrp] - (tid - tile_off[tgrp]) * tm, 0, tm), 0).astype(jnp.int32)
    step_c = jnp.arange(EXPERTS_PER_GROUP, dtype=jnp.int32)
    eidx = tgrp[:, None] * EXPERTS_PER_GROUP + step_c[None, :]
    last_used = jnp.maximum(tile_off[ng] - 1, 0)
    last_e = tgrp[last_used] * EXPERTS_PER_GROUP + EXPERTS_PER_GROUP - 1
    eidx = jnp.where(used[:, None], eidx, last_e).reshape(-1).astype(jnp.int32)
    return src, eidx, nvalid


def _moe(yext, norm2, w_gate, w_up, w_down, norm_final, layer, *, tm, final_norm):
    m = yext.shape[0]
    d = D_MODEL
    n_tiles = m // tm + N_EXPERT_GROUPS
    gidx = yext[:, D_MODEL].astype(jnp.int32)
    src, eidx, nvalid = _route(gidx, tm, n_tiles)
    vec = lambda a: a.reshape(a.shape[0], 1, a.shape[-1])
    wmap = lambda i, c, src, eidx, nvalid: (layer, eidx[i * EXPERTS_PER_GROUP + c], 0, 0)
    est = (tm * EXT_WIDTH * 4 + tm * d * 2 + 2 * tm * d * 4 + 2 * 3 * d * EXPERT_FF * 4
           + 3 * d * EXPERT_FF * 2 + 4 * tm * EXPERT_FF * 4 + tm * d * 4)
    grid_spec = pltpu.PrefetchScalarGridSpec(
        num_scalar_prefetch=3,
        grid=(n_tiles, EXPERTS_PER_GROUP),
        in_specs=[
            pl.BlockSpec(memory_space=pl.ANY),
            pl.BlockSpec((None, 1, d), lambda i, c, *_: (layer, 0, 0)),
            pl.BlockSpec((None, None, d, EXPERT_FF), wmap),
            pl.BlockSpec((None, None, d, EXPERT_FF), wmap),
            pl.BlockSpec((None, None, EXPERT_FF, d), wmap),
            pl.BlockSpec((1, d), lambda i, c, *_: (0, 0)),
        ],
        out_specs=pl.BlockSpec(memory_space=pl.ANY),
        scratch_shapes=[
            pltpu.VMEM((tm, EXT_WIDTH), F32),
            pltpu.VMEM((tm, d), BF16),
            pltpu.VMEM((tm, d), F32),
            pltpu.VMEM((tm, d), F32),
            pltpu.SemaphoreType.DMA,
            pltpu.SemaphoreType.DMA,
        ],
    )
    return pl.pallas_call(
        functools.partial(_moe_kernel, tm=tm, final_norm=final_norm),
        grid_spec=grid_spec,
        out_shape=jax.ShapeDtypeStruct((m, d), F32),
        compiler_params=pltpu.CompilerParams(
            dimension_semantics=("arbitrary", "arbitrary"), vmem_limit_bytes=_vmem_limit(est + (8 << 20))),
        name="moe_experts",
    )(src, eidx, nvalid, yext, vec(norm2), w_gate, w_up, w_down, norm_final.reshape(1, d))


def kernel(x_prompt, x_sample, cache_kv_w128, cache_kv_w512, cache_kv_w2048, state_conv, state_pool, norm1, w_in, conv_w, conv_b, ln_g, ln_b, w_a_out, w_b_out, w_c_map, c_scale, w_o, norm2, w_router_group, b_router_group, w_router_expert, b_router_expert, w_gate, w_up, w_down, norm_final):
    nbp, seq, d = x_prompt.shape
    nbs, steps, _ = x_sample.shape
    depth = w_in.shape[0]
    caches = (cache_kv_w128, cache_kv_w512, cache_kv_w2048)
    past_len = PAST_LEN
    for cache, (win, dil) in zip(caches, ATTN_GROUPS):
        assert cache.shape[2] == win and win <= past_len
    assert steps <= min(dil for _, dil in ATTN_GROUPS[1:])

    wa, wb, wc, wo = (w.astype(BF16) for w in (w_a_out, w_b_out, w_c_map, w_o))
    w_r = jnp.concatenate([w_router_group, w_router_expert], axis=-1)
    w_r = jnp.pad(w_r, ((0, 0), (0, 0), (0, ROUTER_LANES - w_r.shape[-1])))
    b_r = jnp.concatenate([b_router_group, b_router_expert], axis=-1)
    b_r = jnp.pad(b_r, ((0, 0), (0, ROUTER_LANES - b_r.shape[-1])))

    cos_p, sin_p = _rope_tables(jnp.arange(seq, dtype=jnp.int32))
    cos_s, sin_s = _rope_tables(past_len + jnp.arange(steps, dtype=jnp.int32))

    xp = x_prompt.reshape(nbp * seq, d)
    xs = x_sample.transpose(1, 0, 2).reshape(steps * nbs, d)
    ms = steps * nbs
    sconv_tm = state_conv.transpose(0, 2, 1, 3)
    spool_tm = state_pool.transpose(0, 2, 1, 3)
    c0 = cache_kv_w128
    c1 = cache_kv_w512.reshape(depth, nbs, NK, ATTN_GROUPS[1][1], 2, HEADS_PER_GROUP, HEAD_DIM)
    c2 = cache_kv_w2048.reshape(depth, nbs, NK, ATTN_GROUPS[2][1], 2, HEADS_PER_GROUP, HEAD_DIM)
    new_caches = _cache_shift(caches, steps=steps)

    hd = (HEADS_PER_GROUP, HEAD_DIM)
    p_conv, p_pool, p_kv = [], [], [[] for _ in ATTN_GROUPS]
    s_conv, s_pool = [], []
    for l in range(depth):
        last = l == depth - 1
        z = _in_proj(_prenorm(xp, norm1, l, tm=1024), w_in, l, tm=1024)
        a_act, cp, nconv, npool = _branch_ac_prompt(z, conv_w, conv_b, ln_g, ln_b, l, nb=nbp, seq=seq)
        o, k0, k1, k2 = _attn_prompt(z, cos_p, sin_p, nb=nbp, seq=seq)
        yext = _merge(xp, a_act, o, cp, z, wa, wb, wc, c_scale, wo, norm2, w_r, b_r, l, tm=256)
        xp = _moe(yext, norm2, w_gate, w_up, w_down, norm_final, l, tm=512, final_norm=last)
        p_conv.append(nconv)
        p_pool.append(npool)
        zv = z[:, Z_QKV + 2 * ATTN_WIDTH:Z_POOL].reshape(nbp, seq, N_GROUPS, *hd)
        for g, (kg, (win, dil)) in enumerate(zip((k0, k1, k2), ATTN_GROUPS)):
            keep = min(win, seq)
            kk = kg.reshape(nbp, seq, *hd)[:, seq - keep:]
            vv = zv[:, seq - keep:, g]
            p_kv[g].append(jnp.stack([kk, vv], axis=2))

        z = _in_proj(_prenorm(xs, norm1, l, tm=ms), w_in, l, tm=ms)
        a_act, cp, nconv, npool = _branch_ac_sample(z, sconv_tm, spool_tm, conv_w, conv_b, ln_g, ln_b, l,
                                                    nb=nbs, steps=steps, past=past_len)
        qkv5 = z[:, Z_QKV:Z_POOL].reshape(steps, nbs, 3 * N_GROUPS, *hd)
        o, *new_caches = _attn_sample(qkv5, cos_s, sin_s, c0, c1, c2, new_caches, l, nb=nbs, steps=steps)
        o = o.reshape(ms, ATTN_OUT_WIDTH).astype(BF16)
        yext = _merge(xs, a_act, o, cp, z, wa, wb, wc, c_scale, wo, norm2, w_r, b_r, l, tm=ms)
        xs = _moe(yext, norm2, w_gate, w_up, w_down, norm_final, l, tm=ms, final_norm=last)
        s_conv.append(nconv.transpose(1, 0, 2))
        s_pool.append(npool.transpose(1, 0, 2))

    y_prompt = xp.reshape(nbp, seq, d)
    y_sample = xs.reshape(steps, nbs, d).transpose(1, 0, 2)
    return (y_prompt, y_sample, jnp.stack(p_conv, axis=0), jnp.stack(p_pool, axis=0),
            jnp.stack(p_kv[0], axis=0), jnp.stack(p_kv[1], axis=0), jnp.stack(p_kv[2], axis=0),
            jnp.stack(s_conv, axis=0), jnp.stack(s_pool, axis=0),
            new_caches[0], new_caches[1], new_caches[2])
```

```python
import functools

import jax
import jax.numpy as jnp
from jax import lax
from jax.experimental import pallas as pl
from jax.experimental.pallas import tpu as pltpu

F32 = jnp.float32
BF16 = jnp.bfloat16

D_MODEL = 2048
HEAD_DIM = 128
HEADS_PER_GROUP = 4
ATTN_GROUPS = ((128, 1), (512, 4), (2048, 16))
N_GROUPS = len(ATTN_GROUPS)
ATTN_WIDTH = HEADS_PER_GROUP * N_GROUPS * HEAD_DIM
ATTN_OUT_WIDTH = HEADS_PER_GROUP * HEAD_DIM
ROPE_THETA = 10000.0
CONV_CH = D_MODEL // 2
DW_CONV_LEN = 31
CONV_HALO = DW_CONV_LEN - 1
POOL_WINDOWS = (2, 4, 8, 16)
POOL_GROUP = D_MODEL // 8
POOL_WIDTH = len(POOL_WINDOWS) * POOL_GROUP
POOL_OUT_GROUP = D_MODEL // len(POOL_WINDOWS)
POOL_HALO = max(POOL_WINDOWS) - 1
N_BRANCHES = 3
N_EXPERT_GROUPS = 4
EXPERTS_PER_GROUP = 4
N_EXPERTS = N_EXPERT_GROUPS * EXPERTS_PER_GROUP
EXPERT_FF = D_MODEL // 4
NORM_EPS = 1e-6
PAST_LEN = 8192
NK = 128

V7X_VMEM_BYTES = 64 * 1024 * 1024
LANES = 128
ROUTER_LANES = LANES
EXT_WIDTH = D_MODEL + ROUTER_LANES
IN_TILE_N = 512
NEG_INF = float("-inf")
ROW_UNROLL = 8
ATTN_BLOCK_UNROLL = 5


def _vmem_limit(nbytes):
    return int(min(V7X_VMEM_BYTES - (6 << 20), max(nbytes, 16 << 20)))


def _rms(x, g):
    return x * lax.rsqrt(jnp.mean(x * x, axis=-1, keepdims=True) + NORM_EPS) * g


def _sigmoid(x):
    return 1.0 / (1.0 + jnp.exp(-x))


def _rope_tables(pos):
    half = HEAD_DIM // 2
    inv = ROPE_THETA ** (-jnp.arange(half, dtype=F32) / half)
    ang = pos.astype(F32)[:, None] * inv[None, :]
    c, s = jnp.cos(ang), jnp.sin(ang)
    return jnp.concatenate([c, c], axis=-1), jnp.concatenate([-s, s], axis=-1)


def _rope(x, cos2, sin2):
    return x * cos2 + pltpu.roll(x, HEAD_DIM // 2, x.ndim - 1) * sin2


Z_QKV = 2 * CONV_CH
Z_POOL = Z_QKV + 3 * ATTN_WIDTH
Z_GATE = Z_POOL + POOL_WIDTH
IN_WIDTH = Z_GATE + N_BRANCHES * D_MODEL


def _prenorm_kernel(x_ref, g_ref, h_ref):
    h_ref[...] = _rms(x_ref[...], g_ref[...]).astype(BF16)


def _prenorm(x, norm_g, layer, *, tm):
    m, d = x.shape
    return pl.pallas_call(
        _prenorm_kernel,
        grid=(m // tm,),
        in_specs=[
            pl.BlockSpec((tm, d), lambda i: (i, 0)),
            pl.BlockSpec((None, 1, d), lambda i: (layer, 0, 0)),
        ],
        out_specs=pl.BlockSpec((tm, d), lambda i: (i, 0)),
        out_shape=jax.ShapeDtypeStruct((m, d), BF16),
        compiler_params=pltpu.CompilerParams(
            dimension_semantics=("parallel",), vmem_limit_bytes=_vmem_limit(4 * tm * d * 4)),
        name="prenorm",
    )(x, norm_g.reshape(norm_g.shape[0], 1, d))


def _in_proj_kernel(h_ref, w_ref, z_ref, wbf, *, tm):
    i = pl.program_id(1)

    @pl.when(i == 0)
    def _():
        wbf[...] = w_ref[...].astype(BF16)

    rows = pl.ds(pl.multiple_of(i * tm, tm), tm)
    z_ref[...] = jnp.dot(h_ref[rows, :], wbf[...], preferred_element_type=F32)


def _in_proj(h, w_in, layer, *, tm):
    m, d = h.shape
    n = w_in.shape[-1]
    tn = IN_TILE_N
    est = m * d * 2 + 2 * d * tn * 4 + d * tn * 2 + 2 * tm * tn * 4
    return pl.pallas_call(
        functools.partial(_in_proj_kernel, tm=tm),
        grid=(n // tn, m // tm),
        in_specs=[
            pl.BlockSpec((m, d), lambda j, i: (0, 0), pipeline_mode=pl.Buffered(1)),
            pl.BlockSpec((None, d, tn), lambda j, i: (layer, 0, j)),
        ],
        out_specs=pl.BlockSpec((tm, tn), lambda j, i: (i, j)),
        out_shape=jax.ShapeDtypeStruct((m, n), F32),
        scratch_shapes=[pltpu.VMEM((d, tn), BF16)],
        compiler_params=pltpu.CompilerParams(
            dimension_semantics=("arbitrary", "arbitrary"), vmem_limit_bytes=_vmem_limit(est + (8 << 20))),
        name="in_proj",
    )(h, w_in)


CONV_PAD = 32
POOL_PAD = 16


def _conv_ln_silu(ubuf, abuf, w_ref, b_ref, lg_ref, lb_ref, t_rows, row_chunk, col_chunk):
    off = CONV_PAD - CONV_HALO
    for r0 in range(0, t_rows, row_chunk):
        for c0 in range(0, CONV_CH, col_chunk):
            acc = jnp.zeros((row_chunk, col_chunk), F32) + b_ref[:, c0:c0 + col_chunk]
            for k in range(DW_CONV_LEN):
                acc = acc + w_ref[k:k + 1, c0:c0 + col_chunk] * ubuf[r0 + off + k:r0 + off + k + row_chunk,
                                                                    c0:c0 + col_chunk]
            abuf[r0:r0 + row_chunk, c0:c0 + col_chunk] = acc
    a = abuf[...]
    mu = jnp.mean(a, axis=-1, keepdims=True)
    xc = a - mu
    var = jnp.mean(xc * xc, axis=-1, keepdims=True)
    y = xc * lax.rsqrt(var + NORM_EPS) * lg_ref[...] + lb_ref[...]
    return y * _sigmoid(y)


def _branch_ac_prompt_kernel(za1_ref, za2_ref, zc0_ref, zc1_ref, w_ref, b_ref, lg_ref, lb_ref,
                             a_ref, cp_ref, nconv_ref, npool_ref, ubuf, abuf, cbuf, *, t_rows):
    t = pl.program_id(1)
    nt = pl.num_programs(1)

    @pl.when(t == 0)
    def _():
        ubuf[0:CONV_PAD, :] = jnp.zeros((CONV_PAD, CONV_CH), F32)
        cbuf[0:POOL_PAD, :] = jnp.zeros((POOL_PAD, POOL_WIDTH), F32)

    ubuf[CONV_PAD:CONV_PAD + t_rows, :] = za1_ref[...] * _sigmoid(za2_ref[...])
    a_ref[...] = _conv_ln_silu(ubuf, abuf, w_ref, b_ref, lg_ref, lb_ref, t_rows, 64, 512).astype(BF16)

    half = POOL_WIDTH // 2
    cbuf[POOL_PAD:POOL_PAD + t_rows, 0:half] = zc0_ref[...]
    cbuf[POOL_PAD:POOL_PAD + t_rows, half:POOL_WIDTH] = zc1_ref[...]
    rc = 128
    for r0 in range(0, t_rows, rc):
        pos = t * t_rows + r0 + lax.broadcasted_iota(jnp.int32, (rc, 1), 0)
        for g, w in enumerate(POOL_WINDOWS):
            cs = slice(g * POOL_GROUP, (g + 1) * POOL_GROUP)
            s = cbuf[POOL_PAD + r0:POOL_PAD + r0 + rc, cs]
            cur = s
            for i in range(1, w):
                s = s + cbuf[POOL_PAD + r0 - i:POOL_PAD + r0 - i + rc, cs]
            cnt = jnp.minimum(w, pos + 1).astype(F32)
            cp_ref[r0:r0 + rc, cs] = (s / cnt - cur).astype(BF16)

    @pl.when(t == nt - 1)
    def _():
        tail = slice(t_rows - CONV_HALO, t_rows)
        nconv_ref[...] = za1_ref[tail, :] * _sigmoid(za2_ref[tail, :])
        npool_ref[...] = cbuf[POOL_PAD + t_rows - POOL_HALO:POOL_PAD + t_rows, :]

    ubuf[0:CONV_PAD, :] = ubuf[t_rows:t_rows + CONV_PAD, :]
    cbuf[0:POOL_PAD, :] = cbuf[t_rows:t_rows + POOL_PAD, :]


def _branch_ac_prompt(z, conv_w, conv_b, ln_g, ln_b, layer, *, nb, seq, t_rows=256):
    m = nb * seq
    nt = seq // t_rows
    half = POOL_WIDTH // 2
    assert Z_POOL % half == 0
    vec = lambda a: a.reshape(a.shape[0], 1, a.shape[-1])
    vspec = pl.BlockSpec((None, 1, CONV_CH), lambda b, t: (layer, 0, 0))
    return pl.pallas_call(
        functools.partial(_branch_ac_prompt_kernel, t_rows=t_rows),
        grid=(nb, nt),
        in_specs=[
            pl.BlockSpec((t_rows, CONV_CH), lambda b, t: (b * nt + t, 0)),
            pl.BlockSpec((t_rows, CONV_CH), lambda b, t: (b * nt + t, 1)),
            pl.BlockSpec((t_rows, half), lambda b, t: (b * nt + t, Z_POOL // half)),
            pl.BlockSpec((t_rows, half), lambda b, t: (b * nt + t, Z_POOL // half + 1)),
            pl.BlockSpec((None, DW_CONV_LEN, CONV_CH), lambda b, t: (layer, 0, 0)),
            vspec, vspec, vspec,
        ],
        out_specs=[
            pl.BlockSpec((t_rows, CONV_CH), lambda b, t: (b * nt + t, 0)),
            pl.BlockSpec((t_rows, POOL_WIDTH), lambda b, t: (b * nt + t, 0)),
            pl.BlockSpec((None, CONV_HALO, CONV_CH), lambda b, t: (b, 0, 0)),
            pl.BlockSpec((None, POOL_HALO, POOL_WIDTH), lambda b, t: (b, 0, 0)),
        ],
        out_shape=[
            jax.ShapeDtypeStruct((m, CONV_CH), BF16),
            jax.ShapeDtypeStruct((m, POOL_WIDTH), BF16),
            jax.ShapeDtypeStruct((nb, CONV_HALO, CONV_CH), F32),
            jax.ShapeDtypeStruct((nb, POOL_HALO, POOL_WIDTH), F32),
        ],
        scratch_shapes=[
            pltpu.VMEM((CONV_PAD + t_rows, CONV_CH), F32),
            pltpu.VMEM((t_rows, CONV_CH), F32),
            pltpu.VMEM((POOL_PAD + t_rows, POOL_WIDTH), F32),
        ],
        compiler_params=pltpu.CompilerParams(
            dimension_semantics=("parallel", "arbitrary"), vmem_limit_bytes=_vmem_limit(32 << 20)),
        name="branch_ac_prompt",
    )(z, z, z, z, conv_w, vec(conv_b), vec(ln_g), vec(ln_b))


def _branch_ac_sample_kernel(za1_ref, za2_ref, zc0_ref, zc1_ref, sconv_ref, spool_ref, w_ref, b_ref, lg_ref, lb_ref,
                             a_ref, cp_ref, nconv_ref, npool_ref, uext, cext, *, nb, steps, past):
    for i in range(CONV_HALO):
        uext[i] = sconv_ref[i]
    for j in range(steps):
        rows = slice(j * nb, (j + 1) * nb)
        uext[CONV_HALO + j] = za1_ref[rows, :] * _sigmoid(za2_ref[rows, :])
    for j in range(steps):
        acc = jnp.zeros((nb, CONV_CH), F32) + b_ref[...]
        for k in range(DW_CONV_LEN):
            acc = acc + w_ref[k:k + 1, :] * uext[j + k]
        mu = jnp.mean(acc, axis=-1, keepdims=True)
        xc = acc - mu
        var = jnp.mean(xc * xc, axis=-1, keepdims=True)
        y = xc * lax.rsqrt(var + NORM_EPS) * lg_ref[...] + lb_ref[...]
        a_ref[j * nb:(j + 1) * nb, :] = (y * _sigmoid(y)).astype(BF16)
    for i in range(CONV_HALO):
        nconv_ref[i] = uext[steps + i]

    for i in range(POOL_HALO):
        cext[i] = spool_ref[i]
    half = POOL_WIDTH // 2
    for j in range(steps):
        cext[POOL_HALO + j, :, 0:half] = zc0_ref[j * nb:(j + 1) * nb, :]
        cext[POOL_HALO + j, :, half:POOL_WIDTH] = zc1_ref[j * nb:(j + 1) * nb, :]
    for j in range(steps):
        for g, w in enumerate(POOL_WINDOWS):
            cs = slice(g * POOL_GROUP, (g + 1) * POOL_GROUP)
            cur = cext[POOL_HALO + j, :, cs]
            s = cur
            for i in range(1, w):
                s = s + cext[POOL_HALO + j - i, :, cs]
            cnt = float(min(w, past + j + 1))
            cp_ref[j * nb:(j + 1) * nb, cs] = (s / cnt - cur).astype(BF16)
    for i in range(POOL_HALO):
        npool_ref[i] = cext[steps + i]


def _branch_ac_sample(z, sconv_tm, spool_tm, conv_w, conv_b, ln_g, ln_b, layer, *, nb, steps, past):
    m = nb * steps
    half = POOL_WIDTH // 2
    vec = lambda a: a.reshape(a.shape[0], 1, a.shape[-1])
    vspec = pl.BlockSpec((None, 1, CONV_CH), lambda i: (layer, 0, 0))
    return pl.pallas_call(
        functools.partial(_branch_ac_sample_kernel, nb=nb, steps=steps, past=past),
        grid=(1,),
        in_specs=[
            pl.BlockSpec((m, CONV_CH), lambda i: (0, 0)),
            pl.BlockSpec((m, CONV_CH), lambda i: (0, 1)),
            pl.BlockSpec((m, half), lambda i: (0, Z_POOL // half)),
            pl.BlockSpec((m, half), lambda i: (0, Z_POOL // half + 1)),
            pl.BlockSpec((None, CONV_HALO, nb, CONV_CH), lambda i: (layer, 0, 0, 0)),
            pl.BlockSpec((None, POOL_HALO, nb, POOL_WIDTH), lambda i: (layer, 0, 0, 0)),
            pl.BlockSpec((None, DW_CONV_LEN, CONV_CH), lambda i: (layer, 0, 0)),
            vspec, vspec, vspec,
        ],
        out_specs=[
            pl.BlockSpec((m, CONV_CH), lambda i: (0, 0)),
            pl.BlockSpec((m, POOL_WIDTH), lambda i: (0, 0)),
            pl.BlockSpec((CONV_HALO, nb, CONV_CH), lambda i: (0, 0, 0)),
            pl.BlockSpec((POOL_HALO, nb, POOL_WIDTH), lambda i: (0, 0, 0)),
        ],
        out_shape=[
            jax.ShapeDtypeStruct((m, CONV_CH), BF16),
            jax.ShapeDtypeStruct((m, POOL_WIDTH), BF16),
            jax.ShapeDtypeStruct((CONV_HALO, nb, CONV_CH), F32),
            jax.ShapeDtypeStruct((POOL_HALO, nb, POOL_WIDTH), F32),
        ],
        scratch_shapes=[
            pltpu.VMEM((CONV_HALO + steps, nb, CONV_CH), F32),
            pltpu.VMEM((POOL_HALO + steps, nb, POOL_WIDTH), F32),
        ],
        compiler_params=pltpu.CompilerParams(vmem_limit_bytes=_vmem_limit(48 << 20)),
        name="branch_ac_sample",
    )(z, z, z, z, sconv_tm, spool_tm, conv_w, vec(conv_b), vec(ln_g), vec(ln_b))


def _attn_prompt_kernel(*refs, seq):
    qkv_refs = refs[0:9]
    cos_ref, sin_ref = refs[9:11]
    o_ref = refs[11]
    ko_refs = refs[12:15]
    qs, og, lse = refs[15:18]
    scale = HEAD_DIM ** -0.5

    cos2 = cos_ref[...]
    sin2 = sin_ref[...]
    for g in range(N_GROUPS):
        qs[g] = _rope(qkv_refs[3 * g][...], cos2, sin2)
        ko_refs[g][...] = _rope(qkv_refs[3 * g + 1][...], cos2, sin2)

    def rows(start, count, dil):
        if dil == 1:
            return pl.ds(start if isinstance(start, int) else pl.multiple_of(start, NK), count)
        return pl.ds(start, count, stride=dil)

    def block(g, dil, q_start, k_start, nkeys):
        k_ref, v_ref = ko_refs[g], qkv_refs[3 * g + 2]
        q = qs[g, rows(q_start, NK, dil), :].astype(BF16)
        k = k_ref[rows(k_start, nkeys, dil), :].astype(BF16)
        v = v_ref[rows(k_start, nkeys, dil), :].astype(BF16)
        s = lax.dot_general(q, k, (((1,), (1,)), ((), ())), preferred_element_type=F32) * scale
        qi = lax.broadcasted_iota(jnp.int32, (NK, nkeys), 0)
        kk = lax.broadcasted_iota(jnp.int32, (NK, nkeys), 1)
        rel = qi + (nkeys - NK) - kk
        s = jnp.where((rel >= 0) & (rel <= NK), s, NEG_INF)
        mx = jnp.max(s, axis=-1, keepdims=True)
        e = jnp.exp(s - mx)
        l = jnp.sum(e, axis=-1, keepdims=True)
        p = (e * (1.0 / l)).astype(BF16)
        og[g, rows(q_start, NK, dil), :] = jnp.dot(p, v, preferred_element_type=F32)
        lse[g, rows(q_start, NK, dil), :] = jnp.broadcast_to(mx + jnp.log(l), (NK, HEAD_DIM))

    for g, (win, dil) in enumerate(ATTN_GROUPS):
        span = NK * dil
        nblk = seq // span

        def per_residue(r, carry, g=g, dil=dil, span=span, nblk=nblk):
            block(g, dil, r, r, NK)

            def per_block(n, c):
                block(g, dil, n * span + r, (n - 1) * span + r, 2 * NK)
                return c

            if nblk > 1:
                lax.fori_loop(1, nblk, per_block, 0, unroll=ATTN_BLOCK_UNROLL if nblk > 4 else True)
            return carry

        if dil == 1:
            per_residue(0, 0)
        else:
            lax.fori_loop(0, dil, per_residue, 0, unroll=(ATTN_BLOCK_UNROLL - 1) if nblk == 1 else 1)

    m = jnp.maximum(jnp.maximum(lse[0], lse[1]), lse[2])
    num = jnp.zeros((seq, HEAD_DIM), F32)
    den = jnp.zeros((seq, HEAD_DIM), F32)
    for g in range(N_GROUPS):
        w = jnp.exp(lse[g] - m)
        num = num + w * og[g]
        den = den + w
    o_ref[...] = (num / den).astype(BF16)


def _attn_prompt(z, cos2, sin2, *, nb, seq):
    m = nb * seq
    for win, dil in ATTN_GROUPS:
        assert win // dil == NK and seq % (NK * dil) == 0
    nh = HEADS_PER_GROUP * N_GROUPS

    def col(kind, g):
        return lambda b, j: (b, Z_QKV // HEAD_DIM + kind * nh + g * HEADS_PER_GROUP + j)

    in_specs = []
    for g in range(N_GROUPS):
        for kind in range(3):
            in_specs.append(pl.BlockSpec((seq, HEAD_DIM), col(kind, g)))
    in_specs += [pl.BlockSpec((seq, HEAD_DIM), lambda b, j: (0, 0))] * 2
    blk = seq * HEAD_DIM * 4
    est = 2 * 9 * blk + 4 * blk + 2 * blk // 2 + 2 * 3 * blk + 9 * blk
    return pl.pallas_call(
        functools.partial(_attn_prompt_kernel, seq=seq),
        grid=(nb, HEADS_PER_GROUP),
        in_specs=in_specs,
        out_specs=[pl.BlockSpec((seq, HEAD_DIM), lambda b, j: (b, j))] * 4,
        out_shape=[jax.ShapeDtypeStruct((m, ATTN_OUT_WIDTH), BF16)]
        + [jax.ShapeDtypeStruct((m, ATTN_OUT_WIDTH), F32)] * 3,
        scratch_shapes=[pltpu.VMEM((N_GROUPS, seq, HEAD_DIM), F32)] * 3,
        compiler_params=pltpu.CompilerParams(
            dimension_semantics=("parallel", "arbitrary"), vmem_limit_bytes=_vmem_limit(est + (8 << 20))),
        name="attn_prompt",
    )(*([z] * 9), cos2, sin2)


def _attn_sample_kernel(qkv_ref, cos_ref, sin_ref, c0_ref, c1_ref, c2_ref, s0_hbm, s1_hbm, s2_hbm,
                        o_ref, n0_ref, n1_ref, n2_ref, *, steps):
    del s0_hbm, s1_hbm, s2_hbm
    new_refs = (n0_ref, n1_ref, n2_ref)
    scale = HEAD_DIM ** -0.5
    ng = N_GROUPS
    q = [[None] * ng for _ in range(steps)]
    kn = [[None] * ng for _ in range(steps)]
    for j in range(steps):
        cos2 = cos_ref[j:j + 1, :]
        sin2 = sin_ref[j:j + 1, :]
        for g in range(ng):
            q[j][g] = _rope(qkv_ref[j, g], cos2, sin2)
            k_new = _rope(qkv_ref[j, ng + g], cos2, sin2)
            kn[j][g] = k_new
            new_refs[g][j, 0] = k_new
            new_refs[g][j, 1] = qkv_ref[j, 2 * ng + g]

    for j in range(steps):
        parts = []
        for g, (win, dil) in enumerate(ATTN_GROUPS):
            if g == 0:
                kc, vc = c0_ref[:, 0], c0_ref[:, 1]
            elif g == 1:
                kc, vc = c1_ref[:, j, 0], c1_ref[:, j, 1]
            else:
                kc, vc = c2_ref[:, j, 0], c2_ref[:, j, 1]
            qv = q[j][g]
            s = jnp.sum(kc * qv[None], axis=-1, keepdims=True) * scale
            if dil == 1:
                ri = lax.broadcasted_iota(jnp.int32, s.shape, 0)
                s = jnp.where(ri >= j, s, NEG_INF)
                new_js = list(range(j + 1))
            else:
                new_js = [j]
            s_new = [jnp.sum(kn[jn][g] * qv, axis=-1, keepdims=True) * scale for jn in new_js]
            mx = jnp.max(s, axis=0)
            for sn in s_new:
                mx = jnp.maximum(mx, sn)
            e = jnp.exp(s - mx[None])
            e_new = [jnp.exp(sn - mx) for sn in s_new]
            l = jnp.sum(e, axis=0)
            for en in e_new:
                l = l + en
            inv_l = 1.0 / l
            acc = jnp.sum((e * inv_l[None]) * vc, axis=0)
            for jn, en in zip(new_js, e_new):
                acc = acc + (en * inv_l) * qkv_ref[jn, 2 * ng + g]
            parts.append((acc, mx + jnp.log(l)))
        m = jnp.maximum(jnp.maximum(parts[0][1], parts[1][1]), parts[2][1])
        num = jnp.zeros((HEADS_PER_GROUP, HEAD_DIM), F32)
        den = jnp.zeros((HEADS_PER_GROUP, 1), F32)
        for o_g, lse_g in parts:
            w = jnp.exp(lse_g - m)
            num = num + w * o_g
            den = den + w
        o_ref[j] = num / den


def _attn_sample(qkv5, cos2, sin2, c0, c1, c2, shifted, layer, *, nb, steps):
    hd = (HEADS_PER_GROUP, HEAD_DIM)

    def newest(arr):
        assert arr.shape[2] % steps == 0
        last = arr.shape[2] // steps - 1
        return pl.BlockSpec((None, None, steps, 2) + hd, lambda b: (layer, b, last, 0, 0, 0))

    return pl.pallas_call(
        functools.partial(_attn_sample_kernel, steps=steps),
        grid=(nb,),
        in_specs=[
            pl.BlockSpec((steps, None, 3 * N_GROUPS) + hd, lambda b: (0, b, 0, 0, 0)),
            pl.BlockSpec((steps, HEAD_DIM), lambda b: (0, 0)),
            pl.BlockSpec((steps, HEAD_DIM), lambda b: (0, 0)),
            pl.BlockSpec((None, None, NK, 2) + hd, lambda b: (layer, b, 0, 0, 0, 0)),
            pl.BlockSpec((None, None, NK, steps, 2) + hd, lambda b: (layer, b, 0, 0, 0, 0, 0)),
            pl.BlockSpec((None, None, NK, steps, 2) + hd, lambda b: (layer, b, 0, 0, 0, 0, 0)),
        ] + [pl.BlockSpec(memory_space=pl.ANY)] * N_GROUPS,
        out_specs=[pl.BlockSpec((steps, None) + hd, lambda b: (0, b, 0, 0))] + [newest(a) for a in shifted],
        out_shape=[jax.ShapeDtypeStruct((steps, nb) + hd, F32)]
        + [jax.ShapeDtypeStruct(a.shape, a.dtype) for a in shifted],
        input_output_aliases={6 + g: 1 + g for g in range(N_GROUPS)},
        compiler_params=pltpu.CompilerParams(
            dimension_semantics=("parallel",), vmem_limit_bytes=_vmem_limit(40 << 20)),
        name="attn_sample",
    )(qkv5, cos2, sin2, c0, c1, c2, *shifted)


CACHE_BLOCK_ROWS = 512


def _cache_shift_kernel(cur_ref, nxt_ref, out_ref, *, steps, rows):
    out_ref[0:rows - steps] = cur_ref[steps:rows]
    out_ref[rows - steps:rows] = nxt_ref[...]


def _cache_shift(cache, *, steps):
    depth, nb, length = cache.shape[:3]
    rows = min(CACHE_BLOCK_ROWS, length)
    assert length % rows == 0 and rows % steps == 0
    tail = cache.shape[3:]
    zeros = (0,) * len(tail)
    last = length // steps - 1
    return pl.pallas_call(
        functools.partial(_cache_shift_kernel, steps=steps, rows=rows),
        grid=(depth, nb, length // rows),
        in_specs=[
            pl.BlockSpec((None, None, rows) + tail, lambda l, b, k: (l, b, k) + zeros),
            pl.BlockSpec((None, None, steps) + tail,
                         lambda l, b, k: (l, b, jnp.minimum((k + 1) * (rows // steps), last)) + zeros),
        ],
        out_specs=pl.BlockSpec((None, None, rows) + tail, lambda l, b, k: (l, b, k) + zeros),
        out_shape=jax.ShapeDtypeStruct(cache.shape, cache.dtype),
        compiler_params=pltpu.CompilerParams(
            dimension_semantics=("parallel", "parallel", "parallel"), vmem_limit_bytes=_vmem_limit(40 << 20)),
        name="cache_shift",
    )(cache, cache)


def _merge_kernel(*refs):
    x_ref, a_ref, o_ref, cp_ref = refs[0:4]
    ng = len(POOL_WINDOWS)
    gate_refs = refs[4:4 + N_BRANCHES * ng]
    wa_ref, wb_ref, wc_ref, cs_ref, wo_ref, n2_ref, wrh_ref, wrl_ref, br_ref, y_ref = refs[4 + N_BRANCHES * ng:]
    a = jnp.dot(a_ref[...], wa_ref[...], preferred_element_type=F32)
    bb = jnp.dot(o_ref[...], wb_ref[...], preferred_element_type=F32)
    cs = cs_ref[...]
    for g in range(ng):
        oc = slice(g * POOL_OUT_GROUP, (g + 1) * POOL_OUT_GROUP)
        cb = jnp.dot(cp_ref[:, g * POOL_GROUP:(g + 1) * POOL_GROUP], wc_ref[g], preferred_element_type=F32)
        y_ref[:, oc] = (_sigmoid(gate_refs[g][...]) * a[:, oc] + _sigmoid(gate_refs[ng + g][...]) * bb[:, oc]
                        + _sigmoid(gate_refs[2 * ng + g][...]) * (cb * cs[:, oc]))
    mrg = y_ref[:, 0:D_MODEL].astype(BF16)
    y = x_ref[...] + jnp.dot(mrg, wo_ref[...], preferred_element_type=F32)
    y_ref[:, 0:D_MODEL] = y

    h2 = _rms(y, n2_ref[...])
    h_hi = h2.astype(BF16)
    h_lo = (h2 - h_hi.astype(F32)).astype(BF16)
    wrh = wrh_ref[...]
    logits = (jnp.dot(h_hi, wrh, preferred_element_type=F32) + jnp.dot(h_lo, wrh, preferred_element_type=F32)
              + jnp.dot(h_hi, wrl_ref[...], preferred_element_type=F32) + br_ref[...])
    lane = lax.broadcasted_iota(jnp.int32, logits.shape, 1).astype(F32)
    far = float(ROUTER_LANES)
    is_g = lane < N_EXPERT_GROUPS
    gmax = jnp.max(jnp.where(is_g, logits, NEG_INF), axis=-1, keepdims=True)
    g_idx = jnp.min(jnp.where(is_g & (logits == gmax), lane, far), axis=-1, keepdims=True)
    p_group = 1.0 / jnp.sum(jnp.where(is_g, jnp.exp(logits - gmax), 0.0), axis=-1, keepdims=True)
    e_lo = N_EXPERT_GROUPS + g_idx * EXPERTS_PER_GROUP
    sel = (lane >= e_lo) & (lane < e_lo + EXPERTS_PER_GROUP)
    t1 = jnp.max(jnp.where(sel, logits, NEG_INF), axis=-1, keepdims=True)
    i1 = jnp.min(jnp.where(sel & (logits == t1), lane, far), axis=-1, keepdims=True)
    sel2 = sel & (lane != i1)
    t2 = jnp.max(jnp.where(sel2, logits, NEG_INF), axis=-1, keepdims=True)
    i2 = jnp.min(jnp.where(sel2 & (logits == t2), lane, far), axis=-1, keepdims=True)
    r = jnp.exp(t2 - t1)
    w1 = p_group / (1.0 + r)
    w2 = p_group * r / (1.0 + r)
    gate = jnp.where(lane == i1, w1, 0.0) + jnp.where(lane == i2, w2, 0.0)
    y_ref[:, D_MODEL:EXT_WIDTH] = jnp.where(lane == 0.0, g_idx, gate)


def _merge(x, a_act, o, cp, z, wa, wb, wc, c_scale, wo, norm2, wr_hi, wr_lo, b_r, layer, *, tm, row0=0):
    m = x.shape[0]
    d = D_MODEL
    ng = len(POOL_WINDOWS)
    assert Z_GATE % POOL_OUT_GROUP == 0 and d == ng * POOL_OUT_GROUP and row0 % tm == 0

    def gate_spec(s, g):
        col = Z_GATE // POOL_OUT_GROUP + s * ng + g
        return pl.BlockSpec((tm, POOL_OUT_GROUP), lambda i: (row0 // tm + i, col))

    const2 = lambda shape: pl.BlockSpec((None,) + shape, lambda i: (layer,) + (0,) * len(shape),
                                        pipeline_mode=pl.Buffered(1))
    vec = lambda a: a.reshape(a.shape[0], 1, a.shape[-1])
    w_bytes = 2 * (CONV_CH * d + ATTN_OUT_WIDTH * d + POOL_WIDTH * POOL_OUT_GROUP + d * d + 2 * d * ROUTER_LANES)
    act_bytes = tm * (d * 4 + CONV_CH * 2 + ATTN_OUT_WIDTH * 2 + POOL_WIDTH * 2 + 3 * d * 4 + EXT_WIDTH * 4)
    return pl.pallas_call(
        _merge_kernel,
        grid=(m // tm,),
        in_specs=[
            pl.BlockSpec((tm, d), lambda i: (i, 0)),
            pl.BlockSpec((tm, CONV_CH), lambda i: (i, 0)),
            pl.BlockSpec((tm, ATTN_OUT_WIDTH), lambda i: (i, 0)),
            pl.BlockSpec((tm, POOL_WIDTH), lambda i: (i, 0)),
        ] + [gate_spec(s, g) for s in range(N_BRANCHES) for g in range(ng)] + [
            const2((CONV_CH, d)),
            const2((ATTN_OUT_WIDTH, d)),
            const2((len(POOL_WINDOWS), POOL_GROUP, POOL_OUT_GROUP)),
            const2((1, d)),
            const2((d, d)),
            const2((1, d)),
            const2((d, ROUTER_LANES)),
            const2((d, ROUTER_LANES)),
            const2((1, ROUTER_LANES)),
        ],
        out_specs=pl.BlockSpec((tm, EXT_WIDTH), lambda i: (i, 0)),
        out_shape=jax.ShapeDtypeStruct((m, EXT_WIDTH), F32),
        compiler_params=pltpu.CompilerParams(
            dimension_semantics=("parallel",), vmem_limit_bytes=_vmem_limit(w_bytes + 2 * act_bytes + (16 << 20))),
        name="merge_out_router",
    )(x, a_act, o, cp, *([z] * (N_BRANCHES * ng)), wa, wb, wc, vec(c_scale), wo, vec(norm2), wr_hi, wr_lo, vec(b_r))


def _moe_kernel(src_ref, eidx_ref, nvalid_ref, yext_hbm, n2_ref, wg_ref, wu_ref, wd_ref, nf_ref, out_hbm,
                xbuf, hbuf, acc, ybuf, sem_in, sem_out, *, tm, final_norm):
    i = pl.program_id(0)
    c = pl.program_id(1)
    nv = nvalid_ref[i]

    def row_in(r, idx):
        return pltpu.make_async_copy(yext_hbm.at[pl.ds(idx, 1)], xbuf.at[pl.ds(r, 1)], sem_in)

    def row_out(r, idx):
        return pltpu.make_async_copy(ybuf.at[pl.ds(r, 1)], out_hbm.at[pl.ds(idx, 1)], sem_out)

    def for_rows(count, fn):
        full = count // ROW_UNROLL

        def chunk(q, carry):
            for u in range(ROW_UNROLL):
                fn(q * ROW_UNROLL + u)
            return carry

        def single(r, carry):
            fn(r)
            return carry

        lax.fori_loop(0, full, chunk, 0)
        lax.fori_loop(full * ROW_UNROLL, count, single, 0)

    @pl.when((i == 0) & (c == 0))
    def _():
        xbuf[...] = jnp.zeros(xbuf.shape, F32)

    @pl.when((c == 0) & (nv > 0))
    def _():
        for_rows(nv, lambda r: row_in(r, src_ref[i * tm + r]).start())
        for_rows(nv, lambda r: row_in(r, 0).wait())
        hbuf[...] = _rms(xbuf[:, 0:D_MODEL], n2_ref[...]).astype(BF16)

    @pl.when(nv > 0)
    def _():
        h = hbuf[...]
        hg = jnp.dot(h, wg_ref[...].astype(BF16), preferred_element_type=F32)
        hu = jnp.dot(h, wu_ref[...].astype(BF16), preferred_element_type=F32)
        gv = xbuf[:, D_MODEL:EXT_WIDTH]
        lane = lax.broadcasted_iota(jnp.int32, gv.shape, 1)
        e = eidx_ref[i * EXPERTS_PER_GROUP + c]
        gate = jnp.sum(jnp.where(lane == N_EXPERT_GROUPS + e, gv, 0.0), axis=-1, keepdims=True)
        act = (hg * _sigmoid(hg) * hu * gate).astype(BF16)
        part = jnp.dot(act, wd_ref[...].astype(BF16), preferred_element_type=F32)

        @pl.when(c == 0)
        def _():
            acc[...] = part

        @pl.when(c > 0)
        def _():
            acc[...] = acc[...] + part

    @pl.when((c == EXPERTS_PER_GROUP - 1) & (nv > 0))
    def _():
        y = xbuf[:, 0:D_MODEL] + acc[...]
        if final_norm:
            y = _rms(y, nf_ref[...])
        ybuf[...] = y

        for_rows(nv, lambda r: row_out(r, src_ref[i * tm + r]).start())
        for_rows(nv, lambda r: row_out(r, 0).wait())


def _route(gidx, tm, n_tiles):
    m = gidx.shape[0]
    ng = N_EXPERT_GROUPS
    oh = (gidx[:, None] == jnp.arange(ng, dtype=jnp.int32)[None, :]).astype(jnp.int32)
    csum = jnp.cumsum(oh, axis=0)
    rank = jnp.sum(oh * (csum - 1), axis=1)
    counts = csum[-1]
    tiles_g = (counts + tm - 1) // tm
    tile_off = jnp.concatenate([jnp.zeros((1,), jnp.int32), jnp.cumsum(tiles_g)]).astype(jnp.int32)
    pos = tile_off[gidx] * tm + rank
    src = jnp.zeros((n_tiles * tm,), jnp.int32).at[pos].set(jnp.arange(m, dtype=jnp.int32))
    tid = jnp.arange(n_tiles, dtype=jnp.int32)
    tgrp = jnp.clip(jnp.sum((tid[:, None] >= tile_off[None, 1:]).astype(jnp.int32), axis=1), 0, ng - 1)
    used = tid < tile_off[ng]
    nvalid = jnp.where(used, jnp.clip(counts[tgrp] - (tid - tile_off[tgrp]) * tm, 0, tm), 0).astype(jnp.int32)
    step_c = jnp.arange(EXPERTS_PER_GROUP, dtype=jnp.int32)
    eidx = tgrp[:, None] * EXPERTS_PER_GROUP + step_c[None, :]
    last_used = jnp.maximum(tile_off[ng] - 1, 0)
    last_e = tgrp[last_used] * EXPERTS_PER_GROUP + EXPERTS_PER_GROUP - 1
    eidx = jnp.where(used[:, None], eidx, last_e).reshape(-1).astype(jnp.int32)
    return src, eidx, nvalid


def _moe(yext, norm2, w_gate, w_up, w_down, norm_final, layer, *, tm, final_norm):
    m = yext.shape[0]
    d = D_MODEL
    n_tiles = m // tm + N_EXPERT_GROUPS
    gidx = yext[:, D_MODEL].astype(jnp.int32)
    src, eidx, nvalid = _route(gidx, tm, n_tiles)
    vec = lambda a: a.reshape(a.shape[0], 1, a.shape[-1])
    wmap = lambda i, c, src, eidx, nvalid: (layer, eidx[i * EXPERTS_PER_GROUP + c], 0, 0)
    est = (tm * EXT_WIDTH * 4 + tm * d * 2 + 2 * tm * d * 4 + 2 * 3 * d * EXPERT_FF * 4
           + 3 * d * EXPERT_FF * 2 + 4 * tm * EXPERT_FF * 4 + tm * d * 4)
    grid_spec = pltpu.PrefetchScalarGridSpec(
        num_scalar_prefetch=3,
        grid=(n_tiles, EXPERTS_PER_GROUP),
        in_specs=[
            pl.BlockSpec(memory_space=pl.ANY),
            pl.BlockSpec((None, 1, d), lambda i, c, *_: (layer, 0, 0)),
            pl.BlockSpec((None, None, d, EXPERT_FF), wmap),
            pl.BlockSpec((None, None, d, EXPERT_FF), wmap),
            pl.BlockSpec((None, None, EXPERT_FF, d), wmap),
            pl.BlockSpec((1, d), lambda i, c, *_: (0, 0)),
        ],
        out_specs=pl.BlockSpec(memory_space=pl.ANY),
        scratch_shapes=[
            pltpu.VMEM((tm, EXT_WIDTH), F32),
            pltpu.VMEM((tm, d), BF16),
            pltpu.VMEM((tm, d), F32),
            pltpu.VMEM((tm, d), F32),
            pltpu.SemaphoreType.DMA,
            pltpu.SemaphoreType.DMA,
        ],
    )
    return pl.pallas_call(
        functools.partial(_moe_kernel, tm=tm, final_norm=final_norm),
        grid_spec=grid_spec,
        out_shape=jax.ShapeDtypeStruct((m, d), F32),
        compiler_params=pltpu.CompilerParams(
            dimension_semantics=("arbitrary", "arbitrary"), vmem_limit_bytes=_vmem_limit(est + (8 << 20))),
        name="moe_experts",
    )(src, eidx, nvalid, yext, vec(norm2), w_gate, w_up, w_down, norm_final.reshape(1, d))


def kernel(x_prompt, x_sample, cache_kv_w128, cache_kv_w512, cache_kv_w2048, state_conv, state_pool, norm1, w_in, conv_w, conv_b, ln_g, ln_b, w_a_out, w_b_out, w_c_map, c_scale, w_o, norm2, w_router_group, b_router_group, w_router_expert, b_router_expert, w_gate, w_up, w_down, norm_final):
    nbp, seq, d = x_prompt.shape
    nbs, steps, _ = x_sample.shape
    depth = w_in.shape[0]
    caches = (cache_kv_w128, cache_kv_w512, cache_kv_w2048)
    past_len = PAST_LEN
    for cache, (win, dil) in zip(caches, ATTN_GROUPS):
        assert cache.shape[2] == win and win <= past_len
    assert steps <= min(dil for _, dil in ATTN_GROUPS[1:])

    wa, wb, wc, wo = (w.astype(BF16) for w in (w_a_out, w_b_out, w_c_map, w_o))
    w_r = jnp.concatenate([w_router_group, w_router_expert], axis=-1)
    w_r = jnp.pad(w_r, ((0, 0), (0, 0), (0, ROUTER_LANES - w_r.shape[-1])))
    wr_hi = w_r.astype(BF16)
    wr_lo = (w_r - wr_hi.astype(F32)).astype(BF16)
    b_r = jnp.concatenate([b_router_group, b_router_expert], axis=-1)
    b_r = jnp.pad(b_r, ((0, 0), (0, ROUTER_LANES - b_r.shape[-1])))

    cos_p, sin_p = _rope_tables(jnp.arange(seq, dtype=jnp.int32))
    cos_s, sin_s = _rope_tables(past_len + jnp.arange(steps, dtype=jnp.int32))

    xp = x_prompt.reshape(nbp * seq, d)
    xs = x_sample.transpose(1, 0, 2).reshape(steps * nbs, d)
    ms = steps * nbs
    sconv_tm = state_conv.transpose(0, 2, 1, 3)
    spool_tm = state_pool.transpose(0, 2, 1, 3)
    c0 = cache_kv_w128
    c1 = cache_kv_w512.reshape(depth, nbs, NK, ATTN_GROUPS[1][1], 2, HEADS_PER_GROUP, HEAD_DIM)
    c2 = cache_kv_w2048.reshape(depth, nbs, NK, ATTN_GROUPS[2][1], 2, HEADS_PER_GROUP, HEAD_DIM)
    new_caches = [_cache_shift(c, steps=steps) for c in caches]

    hd = (HEADS_PER_GROUP, HEAD_DIM)
    p_conv, p_pool, p_kv = [], [], [[] for _ in ATTN_GROUPS]
    s_conv, s_pool = [], []
    for l in range(depth):
        last = l == depth - 1
        z = _in_proj(_prenorm(xp, norm1, l, tm=1024), w_in, l, tm=1024)
        a_act, cp, nconv, npool = _branch_ac_prompt(z, conv_w, conv_b, ln_g, ln_b, l, nb=nbp, seq=seq)
        o, k0, k1, k2 = _attn_prompt(z, cos_p, sin_p, nb=nbp, seq=seq)
        yext = _merge(xp, a_act, o, cp, z, wa, wb, wc, c_scale, wo, norm2, wr_hi, wr_lo, b_r, l, tm=256)
        xp = _moe(yext, norm2, w_gate, w_up, w_down, norm_final, l, tm=512, final_norm=last)
        p_conv.append(nconv)
        p_pool.append(npool)
        zv = z[:, Z_QKV + 2 * ATTN_WIDTH:Z_POOL].reshape(nbp, seq, N_GROUPS, *hd)
        for g, (kg, (win, dil)) in enumerate(zip((k0, k1, k2), ATTN_GROUPS)):
            keep = min(win, seq)
            kk = kg.reshape(nbp, seq, *hd)[:, seq - keep:]
            vv = zv[:, seq - keep:, g]
            p_kv[g].append(jnp.stack([kk, vv], axis=2))

        z = _in_proj(_prenorm(xs, norm1, l, tm=ms), w_in, l, tm=ms)
        a_act, cp, nconv, npool = _branch_ac_sample(z, sconv_tm, spool_tm, conv_w, conv_b, ln_g, ln_b, l,
                                                    nb=nbs, steps=steps, past=past_len)
        qkv5 = z[:, Z_QKV:Z_POOL].reshape(steps, nbs, 3 * N_GROUPS, *hd)
        o, *new_caches = _attn_sample(qkv5, cos_s, sin_s, c0, c1, c2, new_caches, l, nb=nbs, steps=steps)
        o = o.reshape(ms, ATTN_OUT_WIDTH).astype(BF16)
        yext = _merge(xs, a_act, o, cp, z, wa, wb, wc, c_scale, wo, norm2, wr_hi, wr_lo, b_r, l, tm=ms)
        xs = _moe(yext, norm2, w_gate, w_up, w_down, norm_final, l, tm=ms, final_norm=last)
        s_conv.append(nconv.transpose(1, 0, 2))
        s_pool.append(npool.transpose(1, 0, 2))

    y_prompt = xp.reshape(nbp, seq, d)
    y_sample = xs.reshape(steps, nbs, d).transpose(1, 0, 2)
    return (y_prompt, y_sample, jnp.stack(p_conv, axis=0), jnp.stack(p_pool, axis=0),
            jnp.stack(p_kv[0], axis=0), jnp.stack(p_kv[1], axis=0), jnp.stack(p_kv[2], axis=0),
            jnp.stack(s_conv, axis=0), jnp.stack(s_pool, axis=0),
            new_caches[0], new_caches[1], new_caches[2])
```
